```python
import math
import jax
import jax.numpy as jnp
from jax import lax
import numpy as np

D_MODEL = 2048
BATCH = 2
SEQ = 16384
DEPTH = 1
DEC_BATCH = 1
DEC_SEQ = 16384
PAST_LEN = 128

PLE_DIM = 256
GRID_W = 64
HEAD_DIM = 128
N_Q_HEADS = 8
N_KV_HEADS = 2
Q_PER_KV = N_Q_HEADS // N_KV_HEADS
ATTN_WIDTH = N_Q_HEADS * HEAD_DIM
KV_WIDTH = N_KV_HEADS * HEAD_DIM
HYENA_WIDTH = D_MODEL - ATTN_WIDTH
MIX_WIDTH = ATTN_WIDTH + HYENA_WIDTH
IN_WIDTH = ATTN_WIDTH + 2 * KV_WIDTH + 3 * HYENA_WIDTH
ROPE_AXIS_DIM = HEAD_DIM // 2
ROPE_THETA = 10000.0
Q_BLOCK = 128
SHORT_CONV = 3
FILTER_EMB = 33
FILTER_BANDS = (FILTER_EMB - 1) // 2
FILTER_HIDDEN = 64
DECAY_TARGET = 1e-2
FAST_DECAY_PCT = 0.3
SLOW_DECAY_PCT = 1.5
N_EXPERTS = 32
TOP_K = 4
D_FF = D_MODEL
SWIGLU_ALPHA = 1.702
SWIGLU_LIMIT = 7.0
EXPERT_BLOCK = 256
EPS = 1e-6

kernel_name = 'hymba_gqa_hyena_moe_encoder'


def rms_norm(x, g):
    xf = x.astype(jnp.float32)
    y = xf * lax.rsqrt(jnp.mean(xf * xf, axis=-1, keepdims=True) + EPS)
    return (y * g.astype(jnp.float32)).astype(x.dtype)


def grid_angles(L):
    rows = L // GRID_W
    row_idx = jnp.repeat(jnp.arange(rows, dtype=jnp.float32), GRID_W)
    col_idx = jnp.tile(jnp.arange(GRID_W, dtype=jnp.float32), rows)
    inv_freq = ROPE_THETA ** (-jnp.arange(0, ROPE_AXIS_DIM, 2, dtype=jnp.float32) / ROPE_AXIS_DIM)
    return row_idx[:, None] * inv_freq[None, :], col_idx[:, None] * inv_freq[None, :]


def rope_axis(x, ang):
    half = ROPE_AXIS_DIM // 2
    c = jnp.cos(ang)[None, :, None, :]
    s = jnp.sin(ang)[None, :, None, :]
    x1, x2 = x[..., :half], x[..., half:]
    return jnp.concatenate([x1 * c - x2 * s, x2 * c + x1 * s], axis=-1)


def rope_2d(x, ang_row, ang_col):
    xf = x.astype(jnp.float32)
    y = jnp.concatenate([rope_axis(xf[..., :ROPE_AXIS_DIM], ang_row),
                         rope_axis(xf[..., ROPE_AXIS_DIM:], ang_col)], axis=-1)
    return y.astype(x.dtype)


def block_attention(q, k, v):
    B, L = q.shape[0], q.shape[1]
    nb = L // Q_BLOCK
    qb = (q * (HEAD_DIM ** -0.5)).reshape(B, nb, Q_BLOCK, N_KV_HEADS, Q_PER_KV, HEAD_DIM)
    qb = jnp.transpose(qb, (1, 0, 2, 3, 4, 5))

    def one_block(q_blk):
        s = jnp.einsum('bqkgd,bskd->bkgqs', q_blk, k, preferred_element_type=jnp.float32)
        pr = jax.nn.softmax(s, axis=-1).astype(v.dtype)
        return jnp.einsum('bkgqs,bskd->bqkgd', pr, v)

    o = lax.map(one_block, qb)
    return jnp.transpose(o, (1, 0, 2, 3, 4, 5)).reshape(B, L, ATTN_WIDTH)


def centred_short_conv(u, w, b):
    L = u.shape[1]
    pad = SHORT_CONV // 2
    up = jnp.pad(u, ((0, 0), (pad, SHORT_CONV - 1 - pad), (0, 0)))
    y = up[:, 0:L] * w[0]
    for j in range(1, SHORT_CONV):
        y = y + up[:, j:j + L] * w[j]
    return y + b


def implicit_filter(L, w1, b1, w2, b2, w3, b3, freq, w_out, decay):
    f32 = jnp.float32
    t = jnp.linspace(0.0, 1.0, L, dtype=f32)[:, None]
    w = 2.0 * math.pi * jnp.arange(L, dtype=f32) / L
    bands = jnp.linspace(1e-4, FILTER_BANDS - 1, FILTER_BANDS, dtype=f32)
    ang = w[:, None] * bands[None, :]
    z = jnp.concatenate([t, jnp.cos(ang), -jnp.sin(ang)], axis=-1)
    fr = freq.astype(f32)
    h = jnp.sin(fr * (z @ w1.astype(f32) + b1.astype(f32)))
    h = jnp.sin(fr * (h @ w2.astype(f32) + b2.astype(f32)))
    h = jnp.sin(fr * (h @ w3.astype(f32) + b3.astype(f32)))
    h = h @ w_out.astype(f32)
    return h * jnp.exp(-t * jnp.abs(decay.astype(f32)))


def bidirectional_long_conv(u, filt, bias):
    L, C = u.shape[1], u.shape[2]
    hf, hb = filt[:, :C], filt[:, C:]
    kern = jnp.concatenate([hf.at[0].add(hb[0]), jnp.zeros((1, C), jnp.float32), hb[:0:-1]], axis=0)
    uf = u.astype(jnp.float32)
    spec = jnp.fft.rfft(uf, n=2 * L, axis=1) * jnp.fft.rfft(kern, n=2 * L, axis=0)[None]
    y = jnp.fft.irfft(spec, n=2 * L, axis=1)[:, :L]
    return (y + uf * bias.astype(jnp.float32)).astype(u.dtype)


def hyena_group(u, conv_w, conv_b, f_w1, f_b1, f_w2, f_b2, f_w3, f_b3, f_freq, f_wout, f_decay, hy_bias):
    L = u.shape[1]
    uc = centred_short_conv(u, conv_w, conv_b)
    x0 = uc[..., :HYENA_WIDTH]
    x1 = uc[..., HYENA_WIDTH:2 * HYENA_WIDTH]
    v = uc[..., 2 * HYENA_WIDTH:]
    filt = implicit_filter(L, f_w1, f_b1, f_w2, f_b2, f_w3, f_b3, f_freq, f_wout, f_decay)
    z = bidirectional_long_conv(v * x1, filt, hy_bias)
    return z * x0


def mixer(h, ang_row, ang_col, i, prm):
    B, L, _ = h.shape
    proj = h @ prm['w_in'][i]
    o0 = ATTN_WIDTH
    o1 = o0 + KV_WIDTH
    o2 = o1 + KV_WIDTH
    q = proj[..., :o0].reshape(B, L, N_Q_HEADS, HEAD_DIM)
    k = proj[..., o0:o1].reshape(B, L, N_KV_HEADS, HEAD_DIM)
    v = proj[..., o1:o2].reshape(B, L, N_KV_HEADS, HEAD_DIM)
    q = rope_2d(rms_norm(q, prm['q_gain'][i]), ang_row, ang_col)
    k = rope_2d(rms_norm(k, prm['k_gain'][i]), ang_row, ang_col)
    attn = block_attention(q, k, v)
    hy = hyena_group(proj[..., o2:], prm['conv_w'][i], prm['conv_b'][i],
                     prm['filt_w1'][i], prm['filt_b1'][i], prm['filt_w2'][i], prm['filt_b2'][i],
                     prm['filt_w3'][i], prm['filt_b3'][i], prm['filt_freq'][i],
                     prm['filt_w_out'][i], prm['filt_decay'][i], prm['hyena_bias'][i])
    merged = jnp.concatenate([rms_norm(attn, prm['g_attn_out'][i]),
                              rms_norm(hy, prm['g_hyena_out'][i])], axis=-1)
    return merged @ prm['w_out'][i]


def moe_ffn(h, w_router, b_router, w_up, b_up, w_down, b_down):
    B, L, D = h.shape
    T = B * L
    ht = h.reshape(T, D)
    logits = (ht @ w_router + b_router).astype(jnp.float32)
    top_val, top_idx = lax.top_k(logits, TOP_K)
    gate = jax.nn.softmax(top_val, axis=-1)
    n_assign = T * TOP_K
    e_flat = top_idx.reshape(-1).astype(jnp.int32)
    tok_flat = jnp.arange(n_assign, dtype=jnp.int32) // TOP_K
    g_flat = gate.reshape(-1)
    order = jnp.argsort(e_flat)
    e_sorted, tok_sorted, g_sorted = e_flat[order], tok_flat[order], g_flat[order]
    counts = jnp.bincount(e_flat, length=N_EXPERTS)
    start = jnp.cumsum(counts) - counts
    padded = (counts + EXPERT_BLOCK - 1) // EXPERT_BLOCK * EXPERT_BLOCK
    pend = jnp.cumsum(padded)
    pstart = pend - padded
    slot = pstart[e_sorted] + (jnp.arange(n_assign, dtype=jnp.int32) - start[e_sorted])
    n_slots = n_assign + N_EXPERTS * EXPERT_BLOCK
    n_blocks = n_slots // EXPERT_BLOCK
    slot_tok = jnp.zeros((n_slots,), jnp.int32).at[slot].set(tok_sorted)
    slot_gate = jnp.zeros((n_slots,), jnp.float32).at[slot].set(g_sorted)
    block_start = jnp.arange(n_blocks, dtype=jnp.int32) * EXPERT_BLOCK
    block_expert = jnp.minimum(jnp.sum(block_start[:, None] >= pend[None, :], axis=1), N_EXPERTS - 1)
    x_slots = ht[slot_tok].reshape(n_blocks, EXPERT_BLOCK, D)

    def expert_block(args):
        xb, e = args
        a = xb @ w_up[e] + b_up[e]
        glu = jnp.minimum(a[:, :D_FF], SWIGLU_LIMIT)
        lin = jnp.clip(a[:, D_FF:], -SWIGLU_LIMIT, SWIGLU_LIMIT)
        y = glu * jax.nn.sigmoid(SWIGLU_ALPHA * glu) * (lin + 1.0)
        return y @ w_down[e] + b_down[e]

    y_slots = lax.map(expert_block, (x_slots, block_expert)).reshape(n_slots, D)
    contrib = (y_slots.astype(jnp.float32) * slot_gate[:, None]).astype(h.dtype)
    out = jnp.zeros((T, D), h.dtype).at[slot_tok].add(contrib)
    return out.reshape(B, L, D)


def encoder_trunk(x, p, prm):
    L = x.shape[1]
    ang_row, ang_col = grid_angles(L)
    for i in range(DEPTH):
        h = rms_norm(x, prm['g_mix'][i])
        x = x + mixer(h, ang_row, ang_col, i, prm)
        h = rms_norm(x, prm['g_ffn'][i])
        x = x + moe_ffn(h, prm['w_router'][i], prm['b_router'][i], prm['w_up'][i],
                        prm['b_up'][i], prm['w_down'][i], prm['b_down'][i])
        gate = jax.nn.sigmoid(rms_norm(x, prm['g_ple'][i]) @ prm['w_ple_gate'][i])
        x = x + gate * (p[i] @ prm['w_ple_proj'][i])
    return rms_norm(x, prm['g_final'])


def setup_inputs(seed: int = 0) -> dict:
    key = jax.random.key(seed)
    ks = jax.random.split(key, 40)
    f32 = jnp.float32

    def nrm(k, shape, scale):
        return jax.random.normal(k, shape, f32) * scale

    min_decay = math.log(DECAY_TARGET) / SLOW_DECAY_PCT
    max_decay = math.log(DECAY_TARGET) / FAST_DECAY_PCT
    base_decay = jnp.tile(jnp.linspace(min_decay, max_decay, HYENA_WIDTH, dtype=f32), 2)
    return {
        'x_prompt': nrm(ks[0], (BATCH, SEQ, D_MODEL), 1.0),
        'x_sample': nrm(ks[1], (DEC_BATCH, DEC_SEQ, D_MODEL), 1.0),
        'p_prompt': nrm(ks[2], (DEPTH, BATCH, SEQ, PLE_DIM), 1.0),
        'p_sample': nrm(ks[3], (DEPTH, DEC_BATCH, DEC_SEQ, PLE_DIM), 1.0),
        'g_mix': 1.0 + nrm(ks[4], (DEPTH, D_MODEL), 0.05),
        'w_in': nrm(ks[5], (DEPTH, D_MODEL, IN_WIDTH), D_MODEL ** -0.5),
        'q_gain': 1.0 + nrm(ks[6], (DEPTH, HEAD_DIM), 0.05),
        'k_gain': 1.0 + nrm(ks[7], (DEPTH, HEAD_DIM), 0.05),
        'conv_w': nrm(ks[8], (DEPTH, SHORT_CONV, 3 * HYENA_WIDTH), SHORT_CONV ** -0.5),
        'conv_b': nrm(ks[9], (DEPTH, 3 * HYENA_WIDTH), 0.01),
        'filt_w1': nrm(ks[10], (DEPTH, FILTER_EMB, FILTER_HIDDEN), FILTER_EMB ** -0.5),
        'filt_b1': nrm(ks[11], (DEPTH, FILTER_HIDDEN), 0.01),
        'filt_w2': nrm(ks[12], (DEPTH, FILTER_HIDDEN, FILTER_HIDDEN), FILTER_HIDDEN ** -0.5),
        'filt_b2': nrm(ks[13], (DEPTH, FILTER_HIDDEN), 0.01),
        'filt_w3': nrm(ks[14], (DEPTH, FILTER_HIDDEN, FILTER_HIDDEN), FILTER_HIDDEN ** -0.5),
        'filt_b3': nrm(ks[15], (DEPTH, FILTER_HIDDEN), 0.01),
        'filt_freq': 1.0 + nrm(ks[16], (DEPTH, FILTER_HIDDEN), 0.05),
        'filt_w_out': nrm(ks[17], (DEPTH, FILTER_HIDDEN, 2 * HYENA_WIDTH), FILTER_HIDDEN ** -0.5),
        'filt_decay': jnp.broadcast_to(base_decay, (DEPTH, 2 * HYENA_WIDTH)) + nrm(ks[18], (DEPTH, 2 * HYENA_WIDTH), 0.1),
        'hyena_bias': nrm(ks[19], (DEPTH, HYENA_WIDTH), 0.1),
        'g_attn_out': 1.0 + nrm(ks[20], (DEPTH, ATTN_WIDTH), 0.05),
        'g_hyena_out': 1.0 + nrm(ks[21], (DEPTH, HYENA_WIDTH), 0.05),
        'w_out': nrm(ks[22], (DEPTH, MIX_WIDTH, D_MODEL), MIX_WIDTH ** -0.5),
        'g_ffn': 1.0 + nrm(ks[23], (DEPTH, D_MODEL), 0.05),
        'w_router': nrm(ks[24], (DEPTH, D_MODEL, N_EXPERTS), D_MODEL ** -0.5),
        'b_router': nrm(ks[25], (DEPTH, N_EXPERTS), 0.01),
        'w_up': nrm(ks[26], (DEPTH, N_EXPERTS, D_MODEL, 2 * D_FF), D_MODEL ** -0.5),
        'b_up': nrm(ks[27], (DEPTH, N_EXPERTS, 2 * D_FF), 0.01),
        'w_down': nrm(ks[28], (DEPTH, N_EXPERTS, D_FF, D_MODEL), D_FF ** -0.5),
        'b_down': nrm(ks[29], (DEPTH, N_EXPERTS, D_MODEL), 0.01),
        'g_ple': 1.0 + nrm(ks[30], (DEPTH, D_MODEL), 0.05),
        'w_ple_gate': nrm(ks[31], (DEPTH, D_MODEL, D_MODEL), D_MODEL ** -0.5),
        'w_ple_proj': nrm(ks[32], (DEPTH, PLE_DIM, D_MODEL), PLE_DIM ** -0.5),
        'g_final': 1.0 + nrm(ks[33], (D_MODEL,), 0.05),
    }


def reference(x_prompt, x_sample, p_prompt, p_sample, g_mix, w_in, q_gain, k_gain, conv_w, conv_b,
              filt_w1, filt_b1, filt_w2, filt_b2, filt_w3, filt_b3, filt_freq, filt_w_out, filt_decay,
              hyena_bias, g_attn_out, g_hyena_out, w_out, g_ffn, w_router, b_router, w_up, b_up,
              w_down, b_down, g_ple, w_ple_gate, w_ple_proj, g_final):
    prm = {
        'g_mix': g_mix, 'w_in': w_in, 'q_gain': q_gain, 'k_gain': k_gain,
        'conv_w': conv_w, 'conv_b': conv_b,
        'filt_w1': filt_w1, 'filt_b1': filt_b1, 'filt_w2': filt_w2, 'filt_b2': filt_b2,
        'filt_w3': filt_w3, 'filt_b3': filt_b3, 'filt_freq': filt_freq,
        'filt_w_out': filt_w_out, 'filt_decay': filt_decay, 'hyena_bias': hyena_bias,
        'g_attn_out': g_attn_out, 'g_hyena_out': g_hyena_out, 'w_out': w_out,
        'g_ffn': g_ffn, 'w_router': w_router, 'b_router': b_router,
        'w_up': w_up, 'b_up': b_up, 'w_down': w_down, 'b_down': b_down,
        'g_ple': g_ple, 'w_ple_gate': w_ple_gate, 'w_ple_proj': w_ple_proj, 'g_final': g_final,
    }
    y_prompt = encoder_trunk(x_prompt, p_prompt, prm)
    y_sample = encoder_trunk(x_sample, p_sample, prm)
    return (y_prompt, y_sample)
```

```python
import functools
import math

import numpy as np
import jax
import jax.numpy as jnp
from jax import lax
from jax.experimental import pallas as pl
from jax.experimental.pallas import tpu as pltpu

F32 = jnp.float32
BF16 = jnp.bfloat16

HEAD_DIM = 128
N_Q_HEADS = 8
N_KV_HEADS = 2
Q_PER_KV = N_Q_HEADS // N_KV_HEADS
ATTN_WIDTH = N_Q_HEADS * HEAD_DIM
KV_WIDTH = N_KV_HEADS * HEAD_DIM
GRID_W = 64
ROPE_AXIS_DIM = HEAD_DIM // 2
ROPE_THETA = 10000.0
SHORT_CONV = 3
FILTER_EMB = 33
FILTER_BANDS = (FILTER_EMB - 1) // 2
N_EXPERTS = 32
TOP_K = 4
SWIGLU_ALPHA = 1.702
SWIGLU_LIMIT = 7.0
EPS = 1e-6
LOG2E = 1.4426950408889634

LANES = 128
SUBLANES = 8
VMEM_LIMIT_BYTES = 56 * 1024 * 1024

FFT_N2 = 128
EXPERT_TILE = 1024
FF_TILE = 512


def _params(*sem):
    return pltpu.CompilerParams(dimension_semantics=sem, vmem_limit_bytes=VMEM_LIMIT_BYTES)


def _pick(n, want):
    t = min(n, want)
    while n % t:
        t //= 2
    return t


def _rms(x, g):
    return x * lax.rsqrt(jnp.mean(x * x, axis=-1, keepdims=True) + EPS) * g


def _norm_matmul_kernel(x_ref, g_ref, w_ref, o_ref, h_sc):
    @pl.when(pl.program_id(1) == 0)
    def _():
        h_sc[...] = _rms(x_ref[...], g_ref[...]).astype(BF16)

    o_ref[...] = jnp.dot(h_sc[...], w_ref[...], preferred_element_type=F32)


def norm_matmul(x, g, w, tm=1024, tn=512):
    T, D = x.shape
    N = w.shape[1]
    tm, tn = _pick(T, tm), _pick(N, tn)
    return pl.pallas_call(
        _norm_matmul_kernel,
        grid=(T // tm, N // tn),
        in_specs=[pl.BlockSpec((tm, D), lambda i, j: (i, 0)),
                  pl.BlockSpec((1, D), lambda i, j: (0, 0)),
                  pl.BlockSpec((D, tn), lambda i, j: (0, j))],
        out_specs=pl.BlockSpec((tm, tn), lambda i, j: (i, j)),
        out_shape=jax.ShapeDtypeStruct((T, N), F32),
        scratch_shapes=[pltpu.VMEM((tm, D), BF16)],
        compiler_params=_params("parallel", "arbitrary"),
        name="norm_matmul",
    )(x, g.reshape(1, D), w)


def _qkv_prep_kernel(p_ref, qg_ref, kg_ref, c_ref, s_ref, q_ref, k_ref, v_ref):
    cos = c_ref[...]
    sin = s_ref[...]
    lane = lax.broadcasted_iota(jnp.int32, cos.shape, 1)
    low_half = (lane % ROPE_AXIS_DIM) < (ROPE_AXIS_DIM // 2)

    def norm_rope(x, g):
        y = _rms(x, g)
        fwd = pltpu.roll(y, ROPE_AXIS_DIM // 2, axis=1)
        bwd = pltpu.roll(y, HEAD_DIM - ROPE_AXIS_DIM // 2, axis=1)
        return y * cos + jnp.where(low_half, bwd, fwd) * sin

    q_scale = (HEAD_DIM ** -0.5) * LOG2E
    for h in range(N_Q_HEADS):
        sl = slice(h * HEAD_DIM, (h + 1) * HEAD_DIM)
        q_ref[:, sl] = (norm_rope(p_ref[:, sl], qg_ref[...]) * q_scale).astype(BF16)
    for h in range(N_KV_HEADS):
        src = slice(ATTN_WIDTH + h * HEAD_DIM, ATTN_WIDTH + (h + 1) * HEAD_DIM)
        dst = slice(h * HEAD_DIM, (h + 1) * HEAD_DIM)
        k_ref[:, dst] = norm_rope(p_ref[:, src], kg_ref[...]).astype(BF16)
    v_ref[...] = p_ref[:, ATTN_WIDTH + KV_WIDTH:].astype(BF16)


def qkv_prep(proj, q_gain, k_gain, cos_t, sin_t, L, tl=512):
    T = proj.shape[0]
    tl = _pick(L, tl)
    nl = L // tl
    w_qkv = ATTN_WIDTH + 2 * KV_WIDTH
    return pl.pallas_call(
        _qkv_prep_kernel,
        grid=(T // tl,),
        in_specs=[pl.BlockSpec((tl, w_qkv), lambda i: (i, 0)),
                  pl.BlockSpec((1, HEAD_DIM), lambda i: (0, 0)),
                  pl.BlockSpec((1, HEAD_DIM), lambda i: (0, 0)),
                  pl.BlockSpec((tl, HEAD_DIM), lambda i: (i % nl, 0)),
                  pl.BlockSpec((tl, HEAD_DIM), lambda i: (i % nl, 0))],
        out_specs=[pl.BlockSpec((tl, ATTN_WIDTH), lambda i: (i, 0)),
                   pl.BlockSpec((tl, KV_WIDTH), lambda i: (i, 0)),
                   pl.BlockSpec((tl, KV_WIDTH), lambda i: (i, 0))],
        out_shape=[jax.ShapeDtypeStruct((T, ATTN_WIDTH), BF16),
                   jax.ShapeDtypeStruct((T, KV_WIDTH), BF16),
                   jax.ShapeDtypeStruct((T, KV_WIDTH), BF16)],
        compiler_params=_params("parallel"),
        name="qkv_prep",
    )(proj, q_gain.reshape(1, HEAD_DIM), k_gain.reshape(1, HEAD_DIM), cos_t, sin_t)


def rope_tables(L):
    rows = L // GRID_W
    row_idx = jnp.repeat(jnp.arange(rows, dtype=F32), GRID_W)
    col_idx = jnp.tile(jnp.arange(GRID_W, dtype=F32), rows)
    inv_freq = ROPE_THETA ** (-jnp.arange(0, ROPE_AXIS_DIM, 2, dtype=F32) / ROPE_AXIS_DIM)
    ar = row_idx[:, None] * inv_freq[None, :]
    ac = col_idx[:, None] * inv_freq[None, :]
    cos_t = jnp.concatenate([jnp.cos(ar), jnp.cos(ar), jnp.cos(ac), jnp.cos(ac)], axis=-1)
    sin_t = jnp.concatenate([-jnp.sin(ar), jnp.sin(ar), -jnp.sin(ac), jnp.sin(ac)], axis=-1)
    return cos_t, sin_t


def _attn_kernel(q_ref, k_ref, v_ref, o_ref, qs_sc, m_sc, l_sc, acc_sc, *, tq, tk, n_kv):
    for g in range(Q_PER_KV):
        qs_sc[g * tq:(g + 1) * tq, :] = q_ref[0, :, g * HEAD_DIM:(g + 1) * HEAD_DIM]
    m_sc[...] = jnp.full(m_sc.shape, -jnp.inf, F32)
    l_sc[...] = jnp.zeros(l_sc.shape, F32)
    acc_sc[...] = jnp.zeros(acc_sc.shape, F32)

    def body(j, carry):
        off = pl.multiple_of(j * tk, tk)
        k = k_ref[0, pl.ds(off, tk), :]
        v = v_ref[0, pl.ds(off, tk), :]
        s = lax.dot_general(qs_sc[...], k, (((1,), (1,)), ((), ())),
                            preferred_element_type=F32)
        m_prev = m_sc[...]
        m_new = jnp.maximum(m_prev, jnp.max(s, axis=-1, keepdims=True))
        alpha = jnp.exp2(m_prev - m_new)
        p = jnp.exp2(s - m_new)
        l_sc[...] = alpha * l_sc[...] + jnp.sum(p, axis=-1, keepdims=True)
        acc_sc[...] = alpha * acc_sc[...] + jnp.dot(p.astype(BF16), v, preferred_element_type=F32)
        m_sc[...] = m_new
        return carry

    lax.fori_loop(0, n_kv, body, 0)
    out = acc_sc[...] / l_sc[...]
    for g in range(Q_PER_KV):
        o_ref[0, :, g * HEAD_DIM:(g + 1) * HEAD_DIM] = out[g * tq:(g + 1) * tq, :]


def flash_attention(q, k, v, tq=256, tk=512):
    B, L, _ = q.shape
    tq, tk = _pick(L, tq), _pick(L, tk)
    gw = Q_PER_KV * HEAD_DIM
    kern = functools.partial(_attn_kernel, tq=tq, tk=tk, n_kv=L // tk)
    return pl.pallas_call(
        kern,
        grid=(B, N_KV_HEADS, L // tq),
        in_specs=[pl.BlockSpec((1, tq, gw), lambda b, h, i: (b, i, h)),
                  pl.BlockSpec((1, L, HEAD_DIM), lambda b, h, i: (b, 0, h)),
                  pl.BlockSpec((1, L, HEAD_DIM), lambda b, h, i: (b, 0, h))],
        out_specs=pl.BlockSpec((1, tq, gw), lambda b, h, i: (b, i, h)),
        out_shape=jax.ShapeDtypeStruct((B, L, ATTN_WIDTH), F32),
        scratch_shapes=[pltpu.VMEM((Q_PER_KV * tq, HEAD_DIM), BF16),
                        pltpu.VMEM((Q_PER_KV * tq, 1), F32),
                        pltpu.VMEM((Q_PER_KV * tq, 1), F32),
                        pltpu.VMEM((Q_PER_KV * tq, HEAD_DIM), F32)],
        compiler_params=_params("parallel", "parallel", "arbitrary"),
        name="flash_attention",
    )(q, k, v)


def _hyena_pre_kernel(x0_ref, x1_ref, v_ref, x0p_ref, x1p_ref, vp_ref, x0n_ref, x1n_ref, vn_ref,
                      w0_ref, w1_ref, w2_ref, b0_ref, b1_ref, b2_ref, u_ref, x0o_ref, *, n_l):
    i = pl.program_id(0) % n_l
    first = i == 0
    last = i == n_l - 1
    tl = x0_ref.shape[0]
    row = lax.broadcasted_iota(jnp.int32, x0_ref.shape, 0)

    def conv(x_ref, p_ref, n_ref, w_ref, b_ref):
        x = x_ref[...]
        prev_row = jnp.where(first, 0.0, p_ref[SUBLANES - 1:SUBLANES, :])
        next_row = jnp.where(last, 0.0, n_ref[0:1, :])
        xm = jnp.where(row == 0, prev_row, pltpu.roll(x, 1, axis=0))
        xp = jnp.where(row == tl - 1, next_row, pltpu.roll(x, tl - 1, axis=0))
        return xm * w_ref[0:1, :] + x * w_ref[1:2, :] + xp * w_ref[2:3, :] + b_ref[...]

    x0o_ref[...] = conv(x0_ref, x0p_ref, x0n_ref, w0_ref, b0_ref)
    u_ref[...] = conv(v_ref, vp_ref, vn_ref, w2_ref, b2_ref) * conv(x1_ref, x1p_ref, x1n_ref, w1_ref, b1_ref)


def hyena_pre(proj, conv_w, conv_b, L, tl=512, tc=512):
    T = proj.shape[0]
    C = conv_w.shape[1] // 3
    tl, tc = _pick(L, tl), _pick(C, tc)
    n_l = L // tl
    nc = C // tc
    base = (ATTN_WIDTH + 2 * KV_WIDTH) // tc
    r8 = tl // SUBLANES
    n8 = T // SUBLANES

    def main(k):
        return pl.BlockSpec((tl, tc), lambda i, j: (i, base + k * nc + j))

    def prev(k):
        return pl.BlockSpec((SUBLANES, tc), lambda i, j: (jnp.maximum(i * r8 - 1, 0), base + k * nc + j))

    def nxt(k):
        return pl.BlockSpec((SUBLANES, tc), lambda i, j: (jnp.minimum((i + 1) * r8, n8 - 1), base + k * nc + j))

    def wsp(k):
        return pl.BlockSpec((SHORT_CONV, tc), lambda i, j: (0, k * nc + j))

    def bsp(k):
        return pl.BlockSpec((1, tc), lambda i, j: (0, k * nc + j))

    cb = conv_b.reshape(1, 3 * C)
    kern = functools.partial(_hyena_pre_kernel, n_l=n_l)
    return pl.pallas_call(
        kern,
        grid=(T // tl, nc),
        in_specs=[main(0), main(1), main(2), prev(0), prev(1), prev(2), nxt(0), nxt(1), nxt(2),
                  wsp(0), wsp(1), wsp(2), bsp(0), bsp(1), bsp(2)],
        out_specs=[pl.BlockSpec((tl, tc), lambda i, j: (i, j)),
                   pl.BlockSpec((tl, tc), lambda i, j: (i, j))],
        out_shape=[jax.ShapeDtypeStruct((T, C), F32), jax.ShapeDtypeStruct((T, C), F32)],
        compiler_params=_params("parallel", "parallel"),
        name="hyena_pre",
    )(proj, proj, proj, proj, proj, proj, proj, proj, proj,
      conv_w, conv_w, conv_w, cb, cb, cb)


def _filter_kernel(z_ref, t_ref, w1_ref, b1_ref, w2_ref, b2_ref, w3_ref, b3_ref, fr_ref,
                   wo_ref, dec_ref, o_ref):
    hi = lax.Precision.HIGHEST
    fr = fr_ref[...]
    h = jnp.sin(fr * (jnp.dot(z_ref[...], w1_ref[...], precision=hi, preferred_element_type=F32) + b1_ref[...]))
    h = jnp.sin(fr * (jnp.dot(h, w2_ref[...], precision=hi, preferred_element_type=F32) + b2_ref[...]))
    h = jnp.sin(fr * (jnp.dot(h, w3_ref[...], precision=hi, preferred_element_type=F32) + b3_ref[...]))
    taps = jnp.dot(h, wo_ref[...], precision=hi, preferred_element_type=F32)
    o_ref[...] = taps * jnp.exp(-t_ref[...] * jnp.abs(dec_ref[...]))


def filter_features(L):
    t = jnp.linspace(0.0, 1.0, L, dtype=F32)[:, None]
    w = 2.0 * math.pi * jnp.arange(L, dtype=F32) / L
    bands = jnp.linspace(1e-4, FILTER_BANDS - 1, FILTER_BANDS, dtype=F32)
    ang = w[:, None] * bands[None, :]
    z = jnp.concatenate([t, jnp.cos(ang), -jnp.sin(ang)], axis=-1)
    return z, t


def implicit_filter(L, w1, b1, w2, b2, w3, b3, freq, w_out, decay, tl=512, tn=1024):
    z, t = filter_features(L)
    emb, hid = w1.shape
    emb_pad = -(-emb // LANES) * LANES
    z = jnp.pad(z, ((0, 0), (0, emb_pad - emb)))
    w1 = jnp.pad(w1, ((0, emb_pad - emb), (0, 0)))
    W = w_out.shape[1]
    tl, tn = _pick(L, tl), _pick(W, tn)
    row = lambda a: a.reshape(1, -1)
    const = lambda shape: pl.BlockSpec(shape, lambda i, j: (0, 0))
    return pl.pallas_call(
        _filter_kernel,
        grid=(L // tl, W // tn),
        in_specs=[pl.BlockSpec((tl, emb_pad), lambda i, j: (i, 0)),
                  pl.BlockSpec((tl, 1), lambda i, j: (i, 0)),
                  const((emb_pad, hid)), const((1, hid)),
                  const((hid, hid)), const((1, hid)),
                  const((hid, hid)), const((1, hid)),
                  const((1, hid)),
                  pl.BlockSpec((hid, tn), lambda i, j: (0, j)),
                  pl.BlockSpec((1, tn), lambda i, j: (0, j))],
        out_specs=pl.BlockSpec((tl, tn), lambda i, j: (i, j)),
        out_shape=jax.ShapeDtypeStruct((L, W), F32),
        compiler_params=_params("parallel", "parallel"),
        name="implicit_filter",
    )(z, t, w1, row(b1), w2, row(b2), w3, row(b3), row(freq), w_out, row(decay))


def _dft_consts(L):
    N = 2 * L
    N2 = FFT_N2
    N1 = N // N2
    H = N1 // 2
    k1 = np.arange(N1)
    f1 = np.exp(-2j * np.pi * np.outer(k1, k1) / N1)
    fh = f1[:, :H]
    m1_pair = np.block([[fh.real, -fh.imag], [fh.imag, fh.real]])
    m1_single = np.concatenate([fh.real, fh.imag], axis=0)
    m1_full = np.concatenate([f1.real, f1.imag], axis=0)
    n2 = np.arange(N2)
    f2 = np.exp(-2j * np.pi * np.outer(n2, n2) / N2)
    tw = np.exp(-2j * np.pi * np.outer(k1, n2) / N)
    c = lambda a: jnp.asarray(a, F32)
    return dict(N=N, N1=N1, N2=N2, H=H,
                m1_pair=c(m1_pair), m1_single=c(m1_single), m1_full=c(m1_full),
                m3_pair=c(m1_pair.T / N), m3_single=c(m1_single.T / N),
                f2r=c(f2.real), f2i=c(f2.imag),
                twr=c(tw.real).reshape(N1, 1, N2), twi=c(tw.imag).reshape(N1, 1, N2))


def _dft_stage1_kernel(m_ref, x_ref, o_ref):
    rows = m_ref.shape[1]
    m = m_ref[...].astype(BF16)
    for s in range(SUBLANES):
        x = x_ref[:, :, s, :].reshape(rows, x_ref.shape[-1]).astype(BF16)
        r = jnp.dot(m, x, preferred_element_type=F32)
        o_ref[:, :, s, :] = r.reshape(o_ref.shape[0], o_ref.shape[1], o_ref.shape[3])


def dft_stage1(x4, m1, tc=256):
    S, R, N2, C = x4.shape
    N1 = m1.shape[0] // 2
    tc = _pick(C, tc)
    return pl.pallas_call(
        _dft_stage1_kernel,
        grid=(N2 // SUBLANES, C // tc),
        in_specs=[pl.BlockSpec(m1.shape, lambda j, c: (0, 0)),
                  pl.BlockSpec((S, R, SUBLANES, tc), lambda j, c: (0, 0, j, c))],
        out_specs=pl.BlockSpec((2, N1, SUBLANES, tc), lambda j, c: (0, 0, j, c)),
        out_shape=jax.ShapeDtypeStruct((2, N1, N2, C), F32),
        compiler_params=_params("parallel", "parallel"),
        name="dft_stage1",
    )(m1, x4)


def _stage2_matrix(f2r_ref, f2i_ref, twr_ref, twi_ref):
    f2r, f2i = f2r_ref[...], f2i_ref[...]
    twr, twi = twr_ref[...], twi_ref[...]
    er = f2r * twr - f2i * twi
    ei = f2r * twi + f2i * twr
    return jnp.concatenate([jnp.concatenate([er, -ei], axis=1),
                            jnp.concatenate([ei, er], axis=1)], axis=0)


def _dft_stage2_kernel(f2r_ref, f2i_ref, twr_ref, twi_ref, a_ref, o_ref):
    n2, c = a_ref.shape[1], a_ref.shape[2]
    m = _stage2_matrix(f2r_ref, f2i_ref, twr_ref, twi_ref).astype(BF16)
    a = a_ref[...].reshape(2 * n2, c).astype(BF16)
    o_ref[...] = jnp.dot(m, a, preferred_element_type=F32).reshape(o_ref.shape)


def _spec_blocks(N1, N2, C):
    mat = pl.BlockSpec((N2, N2), lambda i: (0, 0))
    tw = pl.BlockSpec((None, 1, N2), lambda i: (i, 0, 0))
    dat = pl.BlockSpec((2, None, N2, C), lambda i: (0, i, 0, 0))
    return mat, tw, dat


def dft_stage2(a, cst):
    _, N1, N2, C = a.shape
    mat, tw, dat = _spec_blocks(N1, N2, C)
    return pl.pallas_call(
        _dft_stage2_kernel,
        grid=(N1,),
        in_specs=[mat, mat, tw, tw, dat],
        out_specs=dat,
        out_shape=jax.ShapeDtypeStruct(a.shape, F32),
        compiler_params=_params("parallel"),
        name="dft_stage2",
    )(cst["f2r"], cst["f2i"], cst["twr"], cst["twi"], a)


def _spectral_kernel(f2r_ref, f2i_ref, twr_ref, twi_ref, a_ref, k_ref, o_ref):
    n2, c = a_ref.shape[1], a_ref.shape[2]
    m = _stage2_matrix(f2r_ref, f2i_ref, twr_ref, twi_ref)
    a = a_ref[...].reshape(2 * n2, c).astype(BF16)
    x = jnp.dot(m.astype(BF16), a, preferred_element_type=F32)
    xr, xi = x[:n2], x[n2:]
    kr, ki = k_ref[0], k_ref[1]
    y = jnp.concatenate([xr * kr - xi * ki, xr * ki + xi * kr], axis=0).astype(BF16)
    o_ref[...] = jnp.dot(m.T.astype(BF16), y, preferred_element_type=F32).reshape(o_ref.shape)


def spectral_multiply(a, kspec, cst):
    _, N1, N2, C = a.shape
    mat, tw, dat = _spec_blocks(N1, N2, C)
    return pl.pallas_call(
        _spectral_kernel,
        grid=(N1,),
        in_specs=[mat, mat, tw, tw, dat, dat],
        out_specs=dat,
        out_shape=jax.ShapeDtypeStruct(a.shape, F32),
        compiler_params=_params("parallel"),
        name="spectral_multiply",
    )(cst["f2r"], cst["f2i"], cst["twr"], cst["twi"], a, kspec)


def _dft_final_kernel(m_ref, b_ref, u_ref, x0_ref, bias_ref, o_ref):
    rows = m_ref.shape[1]
    c = b_ref.shape[-1]
    m = m_ref[...].astype(BF16)
    bias = bias_ref[...]
    for s in range(SUBLANES):
        b = b_ref[:, :, s, :].reshape(rows, c).astype(BF16)
        y = jnp.dot(m, b, preferred_element_type=F32).reshape(o_ref.shape[0], o_ref.shape[1], c)
        o_ref[:, :, s, :] = (y + u_ref[:, :, s, :] * bias) * x0_ref[:, :, s, :]


def dft_final(b, m3, u4, x04, bias, tc=256):
    S, R, N2, C = u4.shape
    N1 = b.shape[1]
    tc = _pick(C, tc)
    seq = pl.BlockSpec((S, R, SUBLANES, tc), lambda j, c: (0, 0, j, c))
    return pl.pallas_call(
        _dft_final_kernel,
        grid=(N2 // SUBLANES, C // tc),
        in_specs=[pl.BlockSpec(m3.shape, lambda j, c: (0, 0)),
                  pl.BlockSpec((2, N1, SUBLANES, tc), lambda j, c: (0, 0, j, c)),
                  seq, seq,
                  pl.BlockSpec((1, tc), lambda j, c: (0, c))],
        out_specs=seq,
        out_shape=jax.ShapeDtypeStruct(u4.shape, F32),
        compiler_params=_params("parallel", "parallel"),
        name="dft_final",
    )(m3, b, u4, x04, bias.reshape(1, C))


def long_conv_gate(u, x0, filt, bias, n_pair):
    B, L, C = u.shape
    cst = _dft_consts(L)
    N1, N2, H = cst["N1"], cst["N2"], cst["H"]
    hf, hb = filt[:, :C], filt[:, C:]
    kern = jnp.concatenate([hf[:1] + hb[:1], hf[1:], jnp.zeros((1, C), F32), hb[:0:-1]], axis=0)
    kspec = dft_stage2(dft_stage1(kern.reshape(1, N1, N2, C), cst["m1_full"]), cst)
    outs = []
    for lo, hi in ((0, n_pair), (n_pair, B)):
        if hi == lo:
            continue
        pair = (hi - lo) == 2
        u4 = u[lo:hi].reshape(hi - lo, H, N2, C)
        x04 = x0[lo:hi].reshape(hi - lo, H, N2, C)
        a = dft_stage1(u4, cst["m1_pair"] if pair else cst["m1_single"])
        bsp = spectral_multiply(a, kspec, cst)
        y = dft_final(bsp, cst["m3_pair"] if pair else cst["m3_single"], u4, x04, bias)
        outs.append(y.reshape(hi - lo, L, C))
    return jnp.concatenate(outs, axis=0) if len(outs) > 1 else outs[0]


def _pack_bf16_pairs(h):
    half = h.shape[1] // 2
    bits = lax.bitcast_convert_type(h.astype(BF16).astype(F32), jnp.uint32)
    return bits[:, :half] | (bits[:, half:] >> 16)


def _unpack_bf16_pairs(w):
    hi = lax.bitcast_convert_type(w & jnp.uint32(0xFFFF0000), F32).astype(BF16)
    lo = lax.bitcast_convert_type(w << 16, F32).astype(BF16)
    return hi, lo


def _post_mixer_kernel(x_ref, a_ref, hy_ref, ga_ref, gh_ref, wo_ref, gf_ref, wr_ref, br_ref,
                       x1_ref, h_ref, idx_ref, gate_ref):
    merged = jnp.concatenate([_rms(a_ref[...], ga_ref[...]), _rms(hy_ref[...], gh_ref[...])], axis=-1)
    x1 = x_ref[...] + jnp.dot(merged.astype(BF16), wo_ref[...], preferred_element_type=F32)
    x1_ref[...] = x1
    h = _rms(x1, gf_ref[...])
    h_ref[...] = _pack_bf16_pairs(h)
    logits = lax.dot_general(wr_ref[...], h, (((1,), (1,)), ((), ())),
                             precision=lax.Precision.HIGHEST, preferred_element_type=F32) + br_ref[...]
    e_iota = lax.broadcasted_iota(jnp.int32, logits.shape, 0).astype(F32)
    vals, idxs = [], []
    cur = logits
    for _ in range(TOP_K):
        m = jnp.max(cur, axis=0, keepdims=True)
        sel = jnp.min(jnp.where(cur == m, e_iota, float(N_EXPERTS)), axis=0, keepdims=True)
        vals.append(m)
        idxs.append(sel)
        cur = jnp.where(e_iota == sel, -jnp.inf, cur)
    ex = [jnp.exp(v - vals[0]) for v in vals]
    den = ex[0] + ex[1] + ex[2] + ex[3]
    idx_ref[...] = jnp.concatenate(idxs, axis=0).astype(jnp.int32)
    gate_ref[...] = jnp.concatenate([e / den for e in ex], axis=0)


def post_mixer(x, attn, hy, g_attn, g_hy, w_out, g_ffn, w_router_t, b_router, tm=512):
    T, D = x.shape
    Wa, Wh = attn.shape[1], hy.shape[1]
    tm = _pick(T, tm)
    const = lambda shape: pl.BlockSpec(shape, lambda i: (0, 0))
    rows = lambda w: pl.BlockSpec((tm, w), lambda i: (i, 0))
    cols = pl.BlockSpec((TOP_K, tm), lambda i: (0, i))
    return pl.pallas_call(
        _post_mixer_kernel,
        grid=(T // tm,),
        in_specs=[rows(D), rows(Wa), rows(Wh), const((1, Wa)), const((1, Wh)), const((Wa + Wh, D)),
                  const((1, D)), const((N_EXPERTS, D)), const((N_EXPERTS, 1))],
        out_specs=[rows(D), rows(D // 2), cols, cols],
        out_shape=[jax.ShapeDtypeStruct((T, D), F32), jax.ShapeDtypeStruct((T, D // 2), jnp.uint32),
                   jax.ShapeDtypeStruct((TOP_K, T), jnp.int32), jax.ShapeDtypeStruct((TOP_K, T), F32)],
        compiler_params=_params("parallel"),
        name="post_mixer",
    )(x, attn, hy, g_attn.reshape(1, Wa), g_hy.reshape(1, Wh), w_out, g_ffn.reshape(1, D),
      w_router_t, b_router.reshape(N_EXPERTS, 1))


def route(top_idx, top_gate, tile):
    K, T = top_idx.shape
    n_assign = K * T
    e_flat = top_idx.reshape(-1)
    tok_flat = jnp.tile(jnp.arange(T, dtype=jnp.int32), K)
    order = jnp.argsort(e_flat)
    e_sorted = e_flat[order]
    counts = jnp.bincount(e_flat, length=N_EXPERTS).astype(jnp.int32)
    start = jnp.cumsum(counts) - counts
    padded = (counts + tile - 1) // tile * tile
    pend = jnp.cumsum(padded)
    pstart = pend - padded
    slot_sorted = pstart[e_sorted] + (jnp.arange(n_assign, dtype=jnp.int32) - start[e_sorted])
    n_blocks = n_assign // tile + N_EXPERTS
    n_slots = n_blocks * tile
    slot_tok = jnp.zeros((n_slots,), jnp.int32).at[slot_sorted].set(tok_flat[order])
    slot_gate = jnp.zeros((n_slots,), F32).at[slot_sorted].set(top_gate.reshape(-1)[order])
    slot_of = jnp.zeros((n_assign,), jnp.int32).at[order].set(slot_sorted).reshape(K, T)
    block_start = jnp.arange(n_blocks, dtype=jnp.int32) * tile
    n_used = (pend[-1] // tile).astype(jnp.int32)
    block_expert = jnp.minimum(jnp.sum(block_start[:, None] >= pend[None, :], axis=1), N_EXPERTS - 1)
    last_expert = block_expert[jnp.maximum(n_used - 1, 0)]
    block_expert = jnp.where(jnp.arange(n_blocks) < n_used, block_expert, last_expert).astype(jnp.int32)
    return slot_tok, slot_gate, slot_of, block_expert, n_used.reshape(1)


def _row_copy(src_ref, src_row, dst_ref, dst_row, sem):
    return pltpu.make_async_copy(src_ref.at[pl.ds(src_row, 1), :], dst_ref.at[pl.ds(dst_row, 1), :], sem)


def _gather_kernel(idx_ref, src_ref, dst_ref, sem, *, rows):
    base = pl.program_id(0) * rows

    def issue(r, c):
        _row_copy(src_ref, idx_ref[0, 0, r], dst_ref, base + r, sem).start()
        return c

    lax.fori_loop(0, rows, issue, 0)

    def drain(r, c):
        _row_copy(src_ref, 0, dst_ref, base + r, sem).wait()
        return c

    lax.fori_loop(0, rows, drain, 0)


def gather_rows(src, idx, rows=512):
    T, W = src.shape
    n = idx.shape[0]
    rows = _pick(n, rows)
    return pl.pallas_call(
        functools.partial(_gather_kernel, rows=rows),
        grid=(n // rows,),
        in_specs=[pl.BlockSpec((1, 1, rows), lambda i: (i, 0, 0), memory_space=pltpu.SMEM),
                  pl.BlockSpec(memory_space=pl.ANY)],
        out_specs=pl.BlockSpec(memory_space=pl.ANY),
        out_shape=jax.ShapeDtypeStruct((n, W), src.dtype),
        scratch_shapes=[pltpu.SemaphoreType.DMA(())],
        compiler_params=pltpu.CompilerParams(dimension_semantics=("arbitrary",), has_side_effects=True),
        name="gather_rows",
    )(idx.reshape(n // rows, 1, rows), src)


def _expert_kernel(be_ref, nu_ref, x_ref, wg_ref, wl_ref, bg_ref, bl_ref, wd_ref, bd_ref, g_ref,
                   o_ref, xs_sc, *, n_f):
    b = pl.program_id(0)
    f = pl.program_id(1)
    half = x_ref.shape[1]
    used = b < nu_ref[0]

    @pl.when(jnp.logical_and(used, f == 0))
    def _():
        hi, lo = _unpack_bf16_pairs(x_ref[...])
        xs_sc[:, :half] = hi
        xs_sc[:, half:] = lo

    @pl.when(used)
    def _():
        x = xs_sc[...]
        glu = jnp.dot(x, wg_ref[0], preferred_element_type=F32) + bg_ref[0]
        lin = jnp.dot(x, wl_ref[0], preferred_element_type=F32) + bl_ref[0]
        glu = jnp.minimum(glu, SWIGLU_LIMIT)
        lin = jnp.clip(lin, -SWIGLU_LIMIT, SWIGLU_LIMIT)
        y = glu * jax.nn.sigmoid(SWIGLU_ALPHA * glu) * (lin + 1.0)
        part = jnp.dot(y.astype(BF16), wd_ref[0], preferred_element_type=F32)

        @pl.when(f == 0)
        def _():
            o_ref[...] = part

        @pl.when(f > 0)
        def _():
            o_ref[...] += part

        @pl.when(f == n_f - 1)
        def _():
            g = g_ref[...]
            for c in range(o_ref.shape[1] // LANES):
                sl = slice(c * LANES, (c + 1) * LANES)
                o_ref[:, sl] = (o_ref[:, sl] + bd_ref[0, :, sl]) * g

    @pl.when(jnp.logical_and(jnp.logical_not(used), f == n_f - 1))
    def _():
        o_ref[...] = jnp.zeros(o_ref.shape, o_ref.dtype)


def expert_mlp(x_slots, block_expert, n_used, w_up, b_up, w_down, b_down, slot_gate,
               tile=EXPERT_TILE, tf=FF_TILE):
    n_slots, half = x_slots.shape
    D = 2 * half
    E, _, two_ff = w_up.shape
    d_ff = two_ff // 2
    tf = _pick(d_ff, tf)
    n_f = d_ff // tf
    n_blocks = n_slots // tile
    gate_rep = jnp.broadcast_to(slot_gate[:, None], (n_slots, LANES))

    def blk(b, nu):
        return jnp.minimum(b, jnp.maximum(nu[0] - 1, 0))

    def ff(b, f, nu):
        return jnp.where(b < nu[0], f, n_f - 1)

    grid_spec = pltpu.PrefetchScalarGridSpec(
        num_scalar_prefetch=2,
        grid=(n_blocks, n_f),
        in_specs=[pl.BlockSpec((tile, half), lambda b, f, be, nu: (blk(b, nu), 0)),
                  pl.BlockSpec((1, D, tf), lambda b, f, be, nu: (be[b], 0, ff(b, f, nu))),
                  pl.BlockSpec((1, D, tf), lambda b, f, be, nu: (be[b], 0, n_f + ff(b, f, nu))),
                  pl.BlockSpec((1, 1, tf), lambda b, f, be, nu: (be[b], 0, ff(b, f, nu))),
                  pl.BlockSpec((1, 1, tf), lambda b, f, be, nu: (be[b], 0, n_f + ff(b, f, nu))),
                  pl.BlockSpec((1, tf, D), lambda b, f, be, nu: (be[b], ff(b, f, nu), 0)),
                  pl.BlockSpec((1, 1, D), lambda b, f, be, nu: (be[b], 0, 0)),
                  pl.BlockSpec((tile, LANES), lambda b, f, be, nu: (blk(b, nu), 0))],
        out_specs=pl.BlockSpec((tile, D), lambda b, f, be, nu: (b, 0)),
        scratch_shapes=[pltpu.VMEM((tile, D), BF16)])
    return pl.pallas_call(
        functools.partial(_expert_kernel, n_f=n_f),
        grid_spec=grid_spec,
        out_shape=jax.ShapeDtypeStruct((n_slots, D), F32),
        compiler_params=_params("arbitrary", "arbitrary"),
        name="expert_mlp",
    )(block_expert, n_used, x_slots, w_up, w_up, b_up.reshape(E, 1, two_ff), b_up.reshape(E, 1, two_ff),
      w_down, b_down.reshape(E, 1, D), gate_rep)


def _final_kernel(slot_ref, y_hbm, x1_ref, p_ref, gp_ref, wg_ref, wp_ref, gf_ref, o_ref, rows_sc, sem,
                  *, tm):
    def issue(r, c):
        for k in range(TOP_K):
            _row_copy(y_hbm, slot_ref[0, k, r], rows_sc.at[k], r, sem).start()
        return c

    lax.fori_loop(0, tm, issue, 0)

    def drain(r, c):
        for k in range(TOP_K):
            _row_copy(y_hbm, 0, rows_sc.at[k], r, sem).wait()
        return c

    lax.fori_loop(0, tm, drain, 0)

    x2 = x1_ref[...] + ((rows_sc[0] + rows_sc[1]) + (rows_sc[2] + rows_sc[3]))
    gate = jax.nn.sigmoid(jnp.dot(_rms(x2, gp_ref[...]).astype(BF16), wg_ref[...],
                                  preferred_element_type=F32))
    x3 = x2 + gate * jnp.dot(p_ref[...].astype(BF16), wp_ref[...], preferred_element_type=F32)
    o_ref[...] = _rms(x3, gf_ref[...])


def combine_ple_final(y_slots, slot_of, x1, p, g_ple, w_gate, w_proj, g_final, tm=256):
    T, D = x1.shape
    P = p.shape[1]
    tm = _pick(T, tm)
    slots = slot_of.reshape(TOP_K, T // tm, tm).transpose(1, 0, 2)
    const = lambda shape: pl.BlockSpec(shape, lambda i: (0, 0))
    return pl.pallas_call(
        functools.partial(_final_kernel, tm=tm),
        grid=(T // tm,),
        in_specs=[pl.BlockSpec((1, TOP_K, tm), lambda i: (i, 0, 0), memory_space=pltpu.SMEM),
                  pl.BlockSpec(memory_space=pl.ANY),
                  pl.BlockSpec((tm, D), lambda i: (i, 0)),
                  pl.BlockSpec((tm, P), lambda i: (i, 0)),
                  const((1, D)), const((D, D)), const((P, D)), const((1, D))],
        out_specs=pl.BlockSpec((tm, D), lambda i: (i, 0)),
        out_shape=jax.ShapeDtypeStruct((T, D), F32),
        scratch_shapes=[pltpu.VMEM((TOP_K, tm, D), F32),
                        pltpu.SemaphoreType.DMA(())],
        compiler_params=_params("arbitrary"),
        name="combine_ple_final",
    )(slots, y_slots, x1, p, g_ple.reshape(1, D), w_gate, w_proj, g_final.reshape(1, D))


def kernel(x_prompt, x_sample, p_prompt, p_sample, g_mix, w_in, q_gain, k_gain, conv_w, conv_b, filt_w1, filt_b1, filt_w2, filt_b2, filt_w3, filt_b3, filt_freq, filt_w_out, filt_decay, hyena_bias, g_attn_out, g_hyena_out, w_out, g_ffn, w_router, b_router, w_up, b_up, w_down, b_down, g_ple, w_ple_gate, w_ple_proj, g_final):
    depth = g_mix.shape[0]
    bp, L, D = x_prompt.shape
    bs = x_sample.shape[0]
    B = bp + bs
    T = B * L
    C = hyena_bias.shape[1]
    x = jnp.concatenate([x_prompt, x_sample], axis=0).reshape(T, D)
    p = jnp.concatenate([p_prompt, p_sample], axis=1)
    cos_t, sin_t = rope_tables(L)
    n_pair = bp if bp == 2 else 0

    for i in range(depth):
        proj = norm_matmul(x, g_mix[i], w_in[i].astype(BF16))
        q, k, v = qkv_prep(proj, q_gain[i], k_gain[i], cos_t, sin_t, L)
        attn = flash_attention(q.reshape(B, L, ATTN_WIDTH), k.reshape(B, L, KV_WIDTH),
                               v.reshape(B, L, KV_WIDTH)).reshape(T, ATTN_WIDTH)
        u, x0 = hyena_pre(proj, conv_w[i], conv_b[i], L)
        filt = implicit_filter(L, filt_w1[i], filt_b1[i], filt_w2[i], filt_b2[i], filt_w3[i], filt_b3[i],
                               filt_freq[i], filt_w_out[i], filt_decay[i])
        hy = long_conv_gate(u.reshape(B, L, C), x0.reshape(B, L, C), filt, hyena_bias[i], n_pair).reshape(T, C)
        x1, h, top_idx, top_gate = post_mixer(x, attn, hy, g_attn_out[i], g_hyena_out[i],
                                              w_out[i].astype(BF16), g_ffn[i], w_router[i].T, b_router[i])
        slot_tok, slot_gate, slot_of, block_expert, n_used = route(top_idx, top_gate, EXPERT_TILE)
        x_slots = gather_rows(h, slot_tok)
        y_slots = expert_mlp(x_slots, block_expert, n_used, w_up[i].astype(BF16), b_up[i],
                             w_down[i].astype(BF16), b_down[i], slot_gate)
        x = combine_ple_final(y_slots, slot_of, x1, p[i].reshape(T, -1), g_ple[i],
                              w_ple_gate[i].astype(BF16), w_ple_proj[i].astype(BF16), g_final)
    assert depth == 1
    y = x.reshape(B, L, D)
    return (y[:bp], y[bp:])
```

```python
import functools
import math

import numpy as np
import jax
import jax.numpy as jnp
from jax import lax
from jax.experimental import pallas as pl
from jax.experimental.pallas import tpu as pltpu

F32 = jnp.float32
BF16 = jnp.bfloat16

HEAD_DIM = 128
N_Q_HEADS = 8
N_KV_HEADS = 2
Q_PER_KV = N_Q_HEADS // N_KV_HEADS
ATTN_WIDTH = N_Q_HEADS * HEAD_DIM
KV_WIDTH = N_KV_HEADS * HEAD_DIM
GRID_W = 64
ROPE_AXIS_DIM = HEAD_DIM // 2
ROPE_THETA = 10000.0
SHORT_CONV = 3
FILTER_EMB = 33
FILTER_BANDS = (FILTER_EMB - 1) // 2
N_EXPERTS = 32
TOP_K = 4
SWIGLU_ALPHA = 1.702
SWIGLU_LIMIT = 7.0
EPS = 1e-6
LOG2E = 1.4426950408889634

LANES = 128
SUBLANES = 8
VMEM_LIMIT_BYTES = 56 * 1024 * 1024

FFT_N2 = 128
EXPERT_TILE = 1024
FF_TILE = 512


def _params(*sem):
    return pltpu.CompilerParams(dimension_semantics=sem, vmem_limit_bytes=VMEM_LIMIT_BYTES)


def _pick(n, want):
    t = min(n, want)
    while n % t:
        t //= 2
    return t


def _rms(x, g):
    return x * lax.rsqrt(jnp.mean(x * x, axis=-1, keepdims=True) + EPS) * g


def _norm_matmul_kernel(x_ref, g_ref, w_ref, o_ref, h_sc):
    @pl.when(pl.program_id(1) == 0)
    def _():
        h_sc[...] = _rms(x_ref[...], g_ref[...]).astype(BF16)

    o_ref[...] = jnp.dot(h_sc[...], w_ref[...], preferred_element_type=F32)


def norm_matmul(x, g, w, tm=1024, tn=512):
    T, D = x.shape
    N = w.shape[1]
    tm, tn = _pick(T, tm), _pick(N, tn)
    return pl.pallas_call(
        _norm_matmul_kernel,
        grid=(T // tm, N // tn),
        in_specs=[pl.BlockSpec((tm, D), lambda i, j: (i, 0)),
                  pl.BlockSpec((1, D), lambda i, j: (0, 0)),
                  pl.BlockSpec((D, tn), lambda i, j: (0, j))],
        out_specs=pl.BlockSpec((tm, tn), lambda i, j: (i, j)),
        out_shape=jax.ShapeDtypeStruct((T, N), F32),
        scratch_shapes=[pltpu.VMEM((tm, D), BF16)],
        compiler_params=_params("parallel", "arbitrary"),
        name="norm_matmul",
    )(x, g.reshape(1, D), w)


def _qkv_prep_kernel(p_ref, qg_ref, kg_ref, c_ref, s_ref, q_ref, k_ref, v_ref):
    cos = c_ref[...]
    sin = s_ref[...]
    lane = lax.broadcasted_iota(jnp.int32, cos.shape, 1)
    low_half = (lane % ROPE_AXIS_DIM) < (ROPE_AXIS_DIM // 2)

    def norm_rope(x, g):
        y = _rms(x, g)
        fwd = pltpu.roll(y, ROPE_AXIS_DIM // 2, axis=1)
        bwd = pltpu.roll(y, HEAD_DIM - ROPE_AXIS_DIM // 2, axis=1)
        return y * cos + jnp.where(low_half, bwd, fwd) * sin

    q_scale = (HEAD_DIM ** -0.5) * LOG2E
    for h in range(N_Q_HEADS):
        sl = slice(h * HEAD_DIM, (h + 1) * HEAD_DIM)
        q_ref[:, sl] = (norm_rope(p_ref[:, sl], qg_ref[...]) * q_scale).astype(BF16)
    for h in range(N_KV_HEADS):
        src = slice(ATTN_WIDTH + h * HEAD_DIM, ATTN_WIDTH + (h + 1) * HEAD_DIM)
        dst = slice(h * HEAD_DIM, (h + 1) * HEAD_DIM)
        k_ref[:, dst] = norm_rope(p_ref[:, src], kg_ref[...]).astype(BF16)
    ones = jnp.ones((p_ref.shape[0], HEAD_DIM), BF16)
    for h in range(N_KV_HEADS):
        src = slice(ATTN_WIDTH + KV_WIDTH + h * HEAD_DIM, ATTN_WIDTH + KV_WIDTH + (h + 1) * HEAD_DIM)
        v_ref[:, 2 * h * HEAD_DIM:(2 * h + 1) * HEAD_DIM] = p_ref[:, src].astype(BF16)
        v_ref[:, (2 * h + 1) * HEAD_DIM:(2 * h + 2) * HEAD_DIM] = ones


def qkv_prep(proj, q_gain, k_gain, cos_t, sin_t, L, tl=512):
    T = proj.shape[0]
    tl = _pick(L, tl)
    nl = L // tl
    w_qkv = ATTN_WIDTH + 2 * KV_WIDTH
    return pl.pallas_call(
        _qkv_prep_kernel,
        grid=(T // tl,),
        in_specs=[pl.BlockSpec((tl, w_qkv), lambda i: (i, 0)),
                  pl.BlockSpec((1, HEAD_DIM), lambda i: (0, 0)),
                  pl.BlockSpec((1, HEAD_DIM), lambda i: (0, 0)),
                  pl.BlockSpec((tl, HEAD_DIM), lambda i: (i % nl, 0)),
                  pl.BlockSpec((tl, HEAD_DIM), lambda i: (i % nl, 0))],
        out_specs=[pl.BlockSpec((tl, ATTN_WIDTH), lambda i: (i, 0)),
                   pl.BlockSpec((tl, KV_WIDTH), lambda i: (i, 0)),
                   pl.BlockSpec((tl, 2 * KV_WIDTH), lambda i: (i, 0))],
        out_shape=[jax.ShapeDtypeStruct((T, ATTN_WIDTH), BF16),
                   jax.ShapeDtypeStruct((T, KV_WIDTH), BF16),
                   jax.ShapeDtypeStruct((T, 2 * KV_WIDTH), BF16)],
        compiler_params=_params("parallel"),
        name="qkv_prep",
    )(proj, q_gain.reshape(1, HEAD_DIM), k_gain.reshape(1, HEAD_DIM), cos_t, sin_t)


def rope_tables(L):
    rows = L // GRID_W
    row_idx = jnp.repeat(jnp.arange(rows, dtype=F32), GRID_W)
    col_idx = jnp.tile(jnp.arange(GRID_W, dtype=F32), rows)
    inv_freq = ROPE_THETA ** (-jnp.arange(0, ROPE_AXIS_DIM, 2, dtype=F32) / ROPE_AXIS_DIM)
    ar = row_idx[:, None] * inv_freq[None, :]
    ac = col_idx[:, None] * inv_freq[None, :]
    cos_t = jnp.concatenate([jnp.cos(ar), jnp.cos(ar), jnp.cos(ac), jnp.cos(ac)], axis=-1)
    sin_t = jnp.concatenate([-jnp.sin(ar), jnp.sin(ar), -jnp.sin(ac), jnp.sin(ac)], axis=-1)
    return cos_t, sin_t


def _attn_kernel(q_ref, k_ref, v_ref, o_ref, m_sc, acc_sc, *, tk, n_kv):
    m_sc[...] = jnp.full(m_sc.shape, -jnp.inf, F32)
    acc_sc[...] = jnp.zeros(acc_sc.shape, F32)

    def body(j, carry):
        off = pl.multiple_of(j * tk, tk)
        k = k_ref[0, pl.ds(off, tk), :]
        v = v_ref[0, pl.ds(off, tk), :]
        for g in range(Q_PER_KV):
            q = q_ref[0, :, g * HEAD_DIM:(g + 1) * HEAD_DIM]
            s = lax.dot_general(q, k, (((1,), (1,)), ((), ())), preferred_element_type=F32)
            m_prev = m_sc[g]
            m_new = jnp.maximum(m_prev, jnp.max(s, axis=-1, keepdims=True))
            alpha = jnp.exp2(m_prev - m_new)
            p = jnp.exp2(s - jnp.tile(m_new, (1, tk // LANES)))
            pv = jnp.dot(p.astype(BF16), v, preferred_element_type=F32)
            acc_sc[g] = jnp.tile(alpha, (1, 2 * HEAD_DIM // LANES)) * acc_sc[g] + pv
            m_sc[g] = m_new
        return carry

    lax.fori_loop(0, n_kv, body, 0)
    for g in range(Q_PER_KV):
        acc = acc_sc[g]
        o_ref[0, :, g * HEAD_DIM:(g + 1) * HEAD_DIM] = acc[:, :HEAD_DIM] / acc[:, HEAD_DIM:]


def flash_attention(q, k, v1, tq=1024, tk=512):
    B, L, _ = q.shape
    tq, tk = _pick(L, tq), _pick(L, tk)
    gw = Q_PER_KV * HEAD_DIM
    kern = functools.partial(_attn_kernel, tk=tk, n_kv=L // tk)
    return pl.pallas_call(
        kern,
        grid=(B, N_KV_HEADS, L // tq),
        in_specs=[pl.BlockSpec((1, tq, gw), lambda b, h, i: (b, i, h)),
                  pl.BlockSpec((1, L, HEAD_DIM), lambda b, h, i: (b, 0, h)),
                  pl.BlockSpec((1, L, 2 * HEAD_DIM), lambda b, h, i: (b, 0, h))],
        out_specs=pl.BlockSpec((1, tq, gw), lambda b, h, i: (b, i, h)),
        out_shape=jax.ShapeDtypeStruct((B, L, ATTN_WIDTH), F32),
        scratch_shapes=[pltpu.VMEM((Q_PER_KV, tq, LANES), F32),
                        pltpu.VMEM((Q_PER_KV, tq, 2 * HEAD_DIM), F32)],
        compiler_params=_params("parallel", "parallel", "arbitrary"),
        name="flash_attention",
    )(q, k, v1)


def _hyena_pre_kernel(x0_ref, x1_ref, v_ref, x0p_ref, x1p_ref, vp_ref, x0n_ref, x1n_ref, vn_ref,
                      w0_ref, w1_ref, w2_ref, b0_ref, b1_ref, b2_ref, u_ref, x0o_ref, *, n_l):
    i = pl.program_id(0) % n_l
    first = i == 0
    last = i == n_l - 1
    tl = x0_ref.shape[0]
    row = lax.broadcasted_iota(jnp.int32, x0_ref.shape, 0)

    def conv(x_ref, p_ref, n_ref, w_ref, b_ref):
        x = x_ref[...]
        prev_row = jnp.where(first, 0.0, p_ref[SUBLANES - 1:SUBLANES, :])
        next_row = jnp.where(last, 0.0, n_ref[0:1, :])
        xm = jnp.where(row == 0, prev_row, pltpu.roll(x, 1, axis=0))
        xp = jnp.where(row == tl - 1, next_row, pltpu.roll(x, tl - 1, axis=0))
        return xm * w_ref[0:1, :] + x * w_ref[1:2, :] + xp * w_ref[2:3, :] + b_ref[...]

    x0o_ref[...] = conv(x0_ref, x0p_ref, x0n_ref, w0_ref, b0_ref)
    u_ref[...] = conv(v_ref, vp_ref, vn_ref, w2_ref, b2_ref) * conv(x1_ref, x1p_ref, x1n_ref, w1_ref, b1_ref)


def hyena_pre(proj, conv_w, conv_b, L, tl=512, tc=512):
    T = proj.shape[0]
    C = conv_w.shape[1] // 3
    tl, tc = _pick(L, tl), _pick(C, tc)
    n_l = L // tl
    nc = C // tc
    base = (ATTN_WIDTH + 2 * KV_WIDTH) // tc
    r8 = tl // SUBLANES
    n8 = T // SUBLANES

    def main(k):
        return pl.BlockSpec((tl, tc), lambda i, j: (i, base + k * nc + j))

    def prev(k):
        return pl.BlockSpec((SUBLANES, tc), lambda i, j: (jnp.maximum(i * r8 - 1, 0), base + k * nc + j))

    def nxt(k):
        return pl.BlockSpec((SUBLANES, tc), lambda i, j: (jnp.minimum((i + 1) * r8, n8 - 1), base + k * nc + j))

    def wsp(k):
        return pl.BlockSpec((SHORT_CONV, tc), lambda i, j: (0, k * nc + j))

    def bsp(k):
        return pl.BlockSpec((1, tc), lambda i, j: (0, k * nc + j))

    cb = conv_b.reshape(1, 3 * C)
    kern = functools.partial(_hyena_pre_kernel, n_l=n_l)
    return pl.pallas_call(
        kern,
        grid=(T // tl, nc),
        in_specs=[main(0), main(1), main(2), prev(0), prev(1), prev(2), nxt(0), nxt(1), nxt(2),
                  wsp(0), wsp(1), wsp(2), bsp(0), bsp(1), bsp(2)],
        out_specs=[pl.BlockSpec((tl, tc), lambda i, j: (i, j)),
                   pl.BlockSpec((tl, tc), lambda i, j: (i, j))],
        out_shape=[jax.ShapeDtypeStruct((T, C), F32), jax.ShapeDtypeStruct((T, C), F32)],
        compiler_params=_params("parallel", "parallel"),
        name="hyena_pre",
    )(proj, proj, proj, proj, proj, proj, proj, proj, proj,
      conv_w, conv_w, conv_w, cb, cb, cb)


def _filter_kernel(z_ref, t_ref, w1_ref, b1_ref, w2_ref, b2_ref, w3_ref, b3_ref, fr_ref,
                   wo_ref, dec_ref, wob_ref, decb_ref, o_ref, *, n_half):
    i = pl.program_id(0)
    hi = lax.Precision.HIGHEST
    fr = fr_ref[...]
    h = jnp.sin(fr * (jnp.dot(z_ref[...], w1_ref[...], precision=hi, preferred_element_type=F32) + b1_ref[...]))
    h = jnp.sin(fr * (jnp.dot(h, w2_ref[...], precision=hi, preferred_element_type=F32) + b2_ref[...]))
    h = jnp.sin(fr * (jnp.dot(h, w3_ref[...], precision=hi, preferred_element_type=F32) + b3_ref[...]))
    taps = jnp.dot(h, wo_ref[...], precision=hi, preferred_element_type=F32)
    o_ref[...] = taps * jnp.exp(-t_ref[...] * jnp.abs(dec_ref[...]))
    first_row = lax.broadcasted_iota(jnp.int32, (SUBLANES, o_ref.shape[1]), 0) == 0

    @pl.when(i == 0)
    def _():
        hb = jnp.dot(h[:SUBLANES], wob_ref[...], precision=hi, preferred_element_type=F32)
        hb = hb * jnp.exp(-t_ref[0:SUBLANES, :] * jnp.abs(decb_ref[...]))
        o_ref[0:SUBLANES, :] = o_ref[0:SUBLANES, :] + jnp.where(first_row, hb, 0.0)

    @pl.when(i == n_half)
    def _():
        o_ref[0:SUBLANES, :] = jnp.where(first_row, 0.0, o_ref[0:SUBLANES, :])


def filter_features(L):
    t = jnp.linspace(0.0, 1.0, L, dtype=F32)[:, None]
    w = 2.0 * math.pi * jnp.arange(L, dtype=F32) / L
    bands = jnp.linspace(1e-4, FILTER_BANDS - 1, FILTER_BANDS, dtype=F32)
    ang = w[:, None] * bands[None, :]
    z = jnp.concatenate([t, jnp.cos(ang), -jnp.sin(ang)], axis=-1)
    return z, t


def long_conv_kernel_rows(L, w1, b1, w2, b2, w3, b3, freq, w_out, decay, tl=512, tn=1024):
    z, t = filter_features(L)
    mirror = lambda a: jnp.concatenate([a, a[:1], a[:0:-1]], axis=0)
    z, t = mirror(z), mirror(t)
    emb, hid = w1.shape
    emb_pad = -(-emb // LANES) * LANES
    z = jnp.pad(z, ((0, 0), (0, emb_pad - emb)))
    w1 = jnp.pad(w1, ((0, emb_pad - emb), (0, 0)))
    C = w_out.shape[1] // 2
    tl, tn = _pick(L, tl), _pick(C, tn)
    n_half, n_c = L // tl, C // tn
    row = lambda a: a.reshape(1, -1)
    const = lambda shape: pl.BlockSpec(shape, lambda i, j: (0, 0))
    side = lambda i, j: (0, jnp.where(i < n_half, j, n_c + j))
    back = lambda i, j: (0, n_c + j)
    return pl.pallas_call(
        functools.partial(_filter_kernel, n_half=n_half),
        grid=(2 * n_half, n_c),
        in_specs=[pl.BlockSpec((tl, emb_pad), lambda i, j: (i, 0)),
                  pl.BlockSpec((tl, 1), lambda i, j: (i, 0)),
                  const((emb_pad, hid)), const((1, hid)),
                  const((hid, hid)), const((1, hid)),
                  const((hid, hid)), const((1, hid)),
                  const((1, hid)),
                  pl.BlockSpec((hid, tn), side), pl.BlockSpec((1, tn), side),
                  pl.BlockSpec((hid, tn), back), pl.BlockSpec((1, tn), back)],
        out_specs=pl.BlockSpec((tl, tn), lambda i, j: (i, j)),
        out_shape=jax.ShapeDtypeStruct((2 * L, C), F32),
        compiler_params=_params("parallel", "parallel"),
        name="implicit_filter",
    )(z, t, w1, row(b1), w2, row(b2), w3, row(b3), row(freq), w_out, row(decay), w_out, row(decay))


def _dft_consts(L):
    N = 2 * L
    N2 = FFT_N2
    N1 = N // N2
    H = N1 // 2
    k1 = np.arange(N1)
    f1 = np.exp(-2j * np.pi * np.outer(k1, k1) / N1)
    fh = f1[:, :H]
    m1_pair = np.block([[fh.real, -fh.imag], [fh.imag, fh.real]])
    m1_single = np.concatenate([fh.real, fh.imag], axis=0)
    m1_full = np.concatenate([f1.real, f1.imag], axis=0)
    n2 = np.arange(N2)
    f2 = np.exp(-2j * np.pi * np.outer(n2, n2) / N2)
    tw = np.exp(-2j * np.pi * np.outer(k1, n2) / N)
    c = lambda a: jnp.asarray(a, F32)
    return dict(N=N, N1=N1, N2=N2, H=H,
                m1_pair=c(m1_pair), m1_single=c(m1_single), m1_full=c(m1_full),
                m3_pair=c(m1_pair.T / N), m3_single=c(m1_single.T / N),
                f2r=c(f2.real), f2i=c(f2.imag),
                twr=c(tw.real).reshape(N1, 1, N2), twi=c(tw.imag).reshape(N1, 1, N2))


def _dft_stage1_kernel(m_ref, x_ref, o_ref):
    rows = m_ref.shape[1]
    m = m_ref[...].astype(BF16)
    for s in range(SUBLANES):
        x = x_ref[:, :, s, :].reshape(rows, x_ref.shape[-1]).astype(BF16)
        r = jnp.dot(m, x, preferred_element_type=F32)
        o_ref[:, :, s, :] = r.reshape(o_ref.shape[0], o_ref.shape[1], o_ref.shape[3])


def dft_stage1(x4, m1, tc=256):
    S, R, N2, C = x4.shape
    N1 = m1.shape[0] // 2
    tc = _pick(C, tc)
    return pl.pallas_call(
        _dft_stage1_kernel,
        grid=(N2 // SUBLANES, C // tc),
        in_specs=[pl.BlockSpec(m1.shape, lambda j, c: (0, 0)),
                  pl.BlockSpec((S, R, SUBLANES, tc), lambda j, c: (0, 0, j, c))],
        out_specs=pl.BlockSpec((2, N1, SUBLANES, tc), lambda j, c: (0, 0, j, c)),
        out_shape=jax.ShapeDtypeStruct((2, N1, N2, C), F32),
        compiler_params=_params("parallel", "parallel"),
        name="dft_stage1",
    )(m1, x4)


def _stage2_matrix(f2r_ref, f2i_ref, twr_ref, twi_ref):
    f2r, f2i = f2r_ref[...], f2i_ref[...]
    twr, twi = twr_ref[...], twi_ref[...]
    er = f2r * twr - f2i * twi
    ei = f2r * twi + f2i * twr
    return jnp.concatenate([jnp.concatenate([er, -ei], axis=1),
                            jnp.concatenate([ei, er], axis=1)], axis=0)


def _dft_stage2_kernel(f2r_ref, f2i_ref, twr_ref, twi_ref, a_ref, o_ref):
    n2, c = a_ref.shape[1], a_ref.shape[2]
    m = _stage2_matrix(f2r_ref, f2i_ref, twr_ref, twi_ref).astype(BF16)
    a = a_ref[...].reshape(2 * n2, c).astype(BF16)
    o_ref[...] = jnp.dot(m, a, preferred_element_type=F32).reshape(o_ref.shape)


def _spec_blocks(N1, N2, C):
    mat = pl.BlockSpec((N2, N2), lambda i: (0, 0))
    tw = pl.BlockSpec((None, 1, N2), lambda i: (i, 0, 0))
    dat = pl.BlockSpec((2, None, N2, C), lambda i: (0, i, 0, 0))
    return mat, tw, dat


def dft_stage2(a, cst):
    _, N1, N2, C = a.shape
    mat, tw, dat = _spec_blocks(N1, N2, C)
    return pl.pallas_call(
        _dft_stage2_kernel,
        grid=(N1,),
        in_specs=[mat, mat, tw, tw, dat],
        out_specs=dat,
        out_shape=jax.ShapeDtypeStruct(a.shape, F32),
        compiler_params=_params("parallel"),
        name="dft_stage2",
    )(cst["f2r"], cst["f2i"], cst["twr"], cst["twi"], a)


def _spectral_kernel(f2r_ref, f2i_ref, twr_ref, twi_ref, a_ref, k_ref, o_ref):
    n2, c = a_ref.shape[1], a_ref.shape[2]
    m = _stage2_matrix(f2r_ref, f2i_ref, twr_ref, twi_ref)
    a = a_ref[...].reshape(2 * n2, c).astype(BF16)
    x = jnp.dot(m.astype(BF16), a, preferred_element_type=F32)
    xr, xi = x[:n2], x[n2:]
    kr, ki = k_ref[0], k_ref[1]
    y = jnp.concatenate([xr * kr - xi * ki, xr * ki + xi * kr], axis=0).astype(BF16)
    o_ref[...] = jnp.dot(m.T.astype(BF16), y, preferred_element_type=F32).reshape(o_ref.shape)


def spectral_multiply(a, kspec, cst):
    _, N1, N2, C = a.shape
    mat, tw, dat = _spec_blocks(N1, N2, C)
    return pl.pallas_call(
        _spectral_kernel,
        grid=(N1,),
        in_specs=[mat, mat, tw, tw, dat, dat],
        out_specs=dat,
        out_shape=jax.ShapeDtypeStruct(a.shape, F32),
        compiler_params=_params("parallel"),
        name="spectral_multiply",
    )(cst["f2r"], cst["f2i"], cst["twr"], cst["twi"], a, kspec)


def _dft_final_kernel(m_ref, b_ref, u_ref, x0_ref, bias_ref, o_ref):
    rows = m_ref.shape[1]
    c = b_ref.shape[-1]
    m = m_ref[...].astype(BF16)
    bias = bias_ref[...]
    for s in range(SUBLANES):
        b = b_ref[:, :, s, :].reshape(rows, c).astype(BF16)
        y = jnp.dot(m, b, preferred_element_type=F32).reshape(o_ref.shape[0], o_ref.shape[1], c)
        o_ref[:, :, s, :] = (y + u_ref[:, :, s, :] * bias) * x0_ref[:, :, s, :]


def dft_final(b, m3, u4, x04, bias, tc=256):
    S, R, N2, C = u4.shape
    N1 = b.shape[1]
    tc = _pick(C, tc)
    seq = pl.BlockSpec((S, R, SUBLANES, tc), lambda j, c: (0, 0, j, c))
    return pl.pallas_call(
        _dft_final_kernel,
        grid=(N2 // SUBLANES, C // tc),
        in_specs=[pl.BlockSpec(m3.shape, lambda j, c: (0, 0)),
                  pl.BlockSpec((2, N1, SUBLANES, tc), lambda j, c: (0, 0, j, c)),
                  seq, seq,
                  pl.BlockSpec((1, tc), lambda j, c: (0, c))],
        out_specs=seq,
        out_shape=jax.ShapeDtypeStruct(u4.shape, F32),
        compiler_params=_params("parallel", "parallel"),
        name="dft_final",
    )(m3, b, u4, x04, bias.reshape(1, C))


def long_conv_gate(u, x0, kern, bias, n_pair):
    B, L, C = u.shape
    cst = _dft_consts(L)
    N1, N2, H = cst["N1"], cst["N2"], cst["H"]
    kspec = dft_stage2(dft_stage1(kern.reshape(1, N1, N2, C), cst["m1_full"]), cst)
    outs = []
    for lo, hi in ((0, n_pair), (n_pair, B)):
        if hi == lo:
            continue
        pair = (hi - lo) == 2
        u4 = u[lo:hi].reshape(hi - lo, H, N2, C)
        x04 = x0[lo:hi].reshape(hi - lo, H, N2, C)
        a = dft_stage1(u4, cst["m1_pair"] if pair else cst["m1_single"])
        bsp = spectral_multiply(a, kspec, cst)
        y = dft_final(bsp, cst["m3_pair"] if pair else cst["m3_single"], u4, x04, bias)
        outs.append(y.reshape(hi - lo, L, C))
    return jnp.concatenate(outs, axis=0) if len(outs) > 1 else outs[0]


def _pack_bf16_pairs(h):
    half = h.shape[1] // 2
    bits = lax.bitcast_convert_type(h.astype(BF16).astype(F32), jnp.uint32)
    return bits[:, :half] | (bits[:, half:] >> 16)


def _unpack_bf16_pairs(w):
    hi = lax.bitcast_convert_type(w & jnp.uint32(0xFFFF0000), F32).astype(BF16)
    lo = lax.bitcast_convert_type(w << 16, F32).astype(BF16)
    return hi, lo


def _post_mixer_kernel(x_ref, a_ref, hy_ref, ga_ref, gh_ref, wo_ref, gf_ref, wr_ref, br_ref,
                       x1_ref, h_ref, idx_ref, gate_ref):
    merged = jnp.concatenate([_rms(a_ref[...], ga_ref[...]), _rms(hy_ref[...], gh_ref[...])], axis=-1)
    x1 = x_ref[...] + jnp.dot(merged.astype(BF16), wo_ref[...], preferred_element_type=F32)
    x1_ref[...] = x1
    h = _rms(x1, gf_ref[...])
    h_ref[...] = _pack_bf16_pairs(h)
    logits = lax.dot_general(wr_ref[...], h, (((1,), (1,)), ((), ())),
                             precision=lax.Precision.HIGHEST, preferred_element_type=F32) + br_ref[...]
    e_iota = lax.broadcasted_iota(jnp.int32, logits.shape, 0).astype(F32)
    vals, idxs = [], []
    cur = logits
    for _ in range(TOP_K):
        m = jnp.max(cur, axis=0, keepdims=True)
        sel = jnp.min(jnp.where(cur == m, e_iota, float(N_EXPERTS)), axis=0, keepdims=True)
        vals.append(m)
        idxs.append(sel)
        cur = jnp.where(e_iota == sel, -jnp.inf, cur)
    ex = [jnp.exp(v - vals[0]) for v in vals]
    den = ex[0] + ex[1] + ex[2] + ex[3]
    idx_ref[...] = jnp.concatenate(idxs, axis=0).astype(jnp.int32)
    gate_ref[...] = jnp.concatenate([e / den for e in ex], axis=0)


def post_mixer(x, attn, hy, g_attn, g_hy, w_out, g_ffn, w_router_t, b_router, tm=512):
    T, D = x.shape
    Wa, Wh = attn.shape[1], hy.shape[1]
    tm = _pick(T, tm)
    const = lambda shape: pl.BlockSpec(shape, lambda i: (0, 0))
    rows = lambda w: pl.BlockSpec((tm, w), lambda i: (i, 0))
    cols = pl.BlockSpec((TOP_K, tm), lambda i: (0, i))
    return pl.pallas_call(
        _post_mixer_kernel,
        grid=(T // tm,),
        in_specs=[rows(D), rows(Wa), rows(Wh), const((1, Wa)), const((1, Wh)), const((Wa + Wh, D)),
                  const((1, D)), const((N_EXPERTS, D)), const((N_EXPERTS, 1))],
        out_specs=[rows(D), rows(D // 2), cols, cols],
        out_shape=[jax.ShapeDtypeStruct((T, D), F32), jax.ShapeDtypeStruct((T, D // 2), jnp.uint32),
                   jax.ShapeDtypeStruct((TOP_K, T), jnp.int32), jax.ShapeDtypeStruct((TOP_K, T), F32)],
        compiler_params=_params("parallel"),
        name="post_mixer",
    )(x, attn, hy, g_attn.reshape(1, Wa), g_hy.reshape(1, Wh), w_out, g_ffn.reshape(1, D),
      w_router_t, b_router.reshape(N_EXPERTS, 1))


def route(top_idx, top_gate, tile):
    K, T = top_idx.shape
    n_assign = K * T
    e_flat = top_idx.reshape(-1)
    pos = jnp.arange(n_assign, dtype=jnp.int32)
    e_sorted, order, gate_sorted = lax.sort((e_flat, pos, top_gate.reshape(-1)), num_keys=1, is_stable=True)
    tok_sorted = order % T
    counts = jnp.sum((e_flat[None, :] == jnp.arange(N_EXPERTS, dtype=jnp.int32)[:, None]).astype(jnp.int32), axis=1)
    start = jnp.cumsum(counts) - counts
    padded = (counts + tile - 1) // tile * tile
    pend = jnp.cumsum(padded)
    pstart = pend - padded
    n_blocks = n_assign // tile + N_EXPERTS
    n_slots = n_blocks * tile
    block_start = jnp.arange(n_blocks, dtype=jnp.int32) * tile
    n_used = (pend[-1] // tile).astype(jnp.int32)
    block_expert = jnp.minimum(jnp.sum(block_start[:, None] >= pend[None, :], axis=1), N_EXPERTS - 1)
    last_expert = block_expert[jnp.maximum(n_used - 1, 0)]
    block_expert = jnp.where(jnp.arange(n_blocks) < n_used, block_expert, last_expert).astype(jnp.int32)
    slot = jnp.arange(n_slots, dtype=jnp.int32)
    slot_e = jnp.repeat(block_expert, tile)
    within = slot - pstart[slot_e]
    valid = jnp.logical_and(within < counts[slot_e], slot < pend[-1])
    src = jnp.clip(start[slot_e] + within, 0, n_assign - 1)
    slot_tok = jnp.where(valid, tok_sorted[src], 0)
    slot_gate = jnp.where(valid, gate_sorted[src], 0.0)
    slot_sorted = pstart[e_sorted] + (pos - start[e_sorted])
    _, slot_of = lax.sort((order, slot_sorted), num_keys=1)
    return slot_tok, slot_gate, slot_of.reshape(K, T), block_expert, n_used.reshape(1)


def _row_copy(src_ref, src_row, dst_ref, dst_row, sem):
    return pltpu.make_async_copy(src_ref.at[pl.ds(src_row, 1), :], dst_ref.at[pl.ds(dst_row, 1), :], sem)


def _gather_kernel(idx_ref, src_ref, dst_ref, sem, *, rows):
    def issue(r, c):
        _row_copy(src_ref, idx_ref[0, 0, r], dst_ref, r, sem).start()
        return c

    lax.fori_loop(0, rows, issue, 0)

    def drain(r, c):
        _row_copy(src_ref, 0, dst_ref, r, sem).wait()
        return c

    lax.fori_loop(0, rows, drain, 0)


def gather_rows(src, idx, rows=512):
    T, W = src.shape
    n = idx.shape[0]
    rows = _pick(n, rows)
    return pl.pallas_call(
        functools.partial(_gather_kernel, rows=rows),
        grid=(n // rows,),
        in_specs=[pl.BlockSpec((1, 1, rows), lambda i: (i, 0, 0), memory_space=pltpu.SMEM),
                  pl.BlockSpec(memory_space=pl.ANY)],
        out_specs=pl.BlockSpec((rows, W), lambda i: (i, 0)),
        out_shape=jax.ShapeDtypeStruct((n, W), src.dtype),
        scratch_shapes=[pltpu.SemaphoreType.DMA(())],
        compiler_params=_params("arbitrary"),
        name="gather_rows",
    )(idx.reshape(n // rows, 1, rows), src)


def _expert_kernel(be_ref, nu_ref, x_ref, wg_ref, wl_ref, bg_ref, bl_ref, wd_ref, bd_ref, g_ref,
                   o_ref, xs_sc, *, n_f):
    b = pl.program_id(0)
    f = pl.program_id(1)
    half = x_ref.shape[1]
    used = b < nu_ref[0]

    @pl.when(jnp.logical_and(used, f == 0))
    def _():
        hi, lo = _unpack_bf16_pairs(x_ref[...])
        xs_sc[:, :half] = hi
        xs_sc[:, half:] = lo

    @pl.when(used)
    def _():
        x = xs_sc[...]
        glu = jnp.dot(x, wg_ref[0], preferred_element_type=F32) + bg_ref[0]
        lin = jnp.dot(x, wl_ref[0], preferred_element_type=F32) + bl_ref[0]
        glu = jnp.minimum(glu, SWIGLU_LIMIT)
        lin = jnp.clip(lin, -SWIGLU_LIMIT, SWIGLU_LIMIT)
        y = glu * jax.nn.sigmoid(SWIGLU_ALPHA * glu) * (lin + 1.0)
        part = jnp.dot(y.astype(BF16), wd_ref[0], preferred_element_type=F32)

        @pl.when(f == 0)
        def _():
            o_ref[...] = part

        @pl.when(f > 0)
        def _():
            o_ref[...] += part

        @pl.when(f == n_f - 1)
        def _():
            g = g_ref[...]
            for c in range(o_ref.shape[1] // LANES):
                sl = slice(c * LANES, (c + 1) * LANES)
                o_ref[:, sl] = (o_ref[:, sl] + bd_ref[0, :, sl]) * g

    @pl.when(jnp.logical_and(jnp.logical_not(used), f == n_f - 1))
    def _():
        o_ref[...] = jnp.zeros(o_ref.shape, o_ref.dtype)


def expert_mlp(x_slots, block_expert, n_used, w_up, b_up, w_down, b_down, slot_gate,
               tile=EXPERT_TILE, tf=FF_TILE):
    n_slots, half = x_slots.shape
    D = 2 * half
    E, _, two_ff = w_up.shape
    d_ff = two_ff // 2
    tf = _pick(d_ff, tf)
    n_f = d_ff // tf
    n_blocks = n_slots // tile
    gate_rep = jnp.broadcast_to(slot_gate[:, None], (n_slots, LANES))

    def blk(b, nu):
        return jnp.minimum(b, jnp.maximum(nu[0] - 1, 0))

    def ff(b, f, nu):
        return jnp.where(b < nu[0], f, n_f - 1)

    grid_spec = pltpu.PrefetchScalarGridSpec(
        num_scalar_prefetch=2,
        grid=(n_blocks, n_f),
        in_specs=[pl.BlockSpec((tile, half), lambda b, f, be, nu: (blk(b, nu), 0)),
                  pl.BlockSpec((1, D, tf), lambda b, f, be, nu: (be[b], 0, ff(b, f, nu))),
                  pl.BlockSpec((1, D, tf), lambda b, f, be, nu: (be[b], 0, n_f + ff(b, f, nu))),
                  pl.BlockSpec((1, 1, tf), lambda b, f, be, nu: (be[b], 0, ff(b, f, nu))),
                  pl.BlockSpec((1, 1, tf), lambda b, f, be, nu: (be[b], 0, n_f + ff(b, f, nu))),
                  pl.BlockSpec((1, tf, D), lambda b, f, be, nu: (be[b], ff(b, f, nu), 0)),
                  pl.BlockSpec((1, 1, D), lambda b, f, be, nu: (be[b], 0, 0)),
                  pl.BlockSpec((tile, LANES), lambda b, f, be, nu: (blk(b, nu), 0))],
        out_specs=pl.BlockSpec((tile, D), lambda b, f, be, nu: (b, 0)),
        scratch_shapes=[pltpu.VMEM((tile, D), BF16)])
    return pl.pallas_call(
        functools.partial(_expert_kernel, n_f=n_f),
        grid_spec=grid_spec,
        out_shape=jax.ShapeDtypeStruct((n_slots, D), F32),
        compiler_params=_params("arbitrary", "arbitrary"),
        name="expert_mlp",
    )(block_expert, n_used, x_slots, w_up, w_up, b_up.reshape(E, 1, two_ff), b_up.reshape(E, 1, two_ff),
      w_down, b_down.reshape(E, 1, D), gate_rep)


def _final_kernel(slot_ref, y_hbm, x1_ref, p_ref, gp_ref, wg_ref, wp_ref, gf_ref, o_ref, rows_sc, sem,
                  *, tm):
    def issue(r, c):
        for k in range(TOP_K):
            _row_copy(y_hbm, slot_ref[0, k, r], rows_sc.at[k], r, sem).start()
        return c

    lax.fori_loop(0, tm, issue, 0)

    def drain(r, c):
        for k in range(TOP_K):
            _row_copy(y_hbm, 0, rows_sc.at[k], r, sem).wait()
        return c

    lax.fori_loop(0, tm, drain, 0)

    x2 = x1_ref[...] + ((rows_sc[0] + rows_sc[1]) + (rows_sc[2] + rows_sc[3]))
    gate = jax.nn.sigmoid(jnp.dot(_rms(x2, gp_ref[...]).astype(BF16), wg_ref[...],
                                  preferred_element_type=F32))
    x3 = x2 + gate * jnp.dot(p_ref[...].astype(BF16), wp_ref[...], preferred_element_type=F32)
    o_ref[...] = _rms(x3, gf_ref[...])


def combine_ple_final(y_slots, slot_of, x1, p, g_ple, w_gate, w_proj, g_final, tm=256):
    T, D = x1.shape
    P = p.shape[1]
    tm = _pick(T, tm)
    slots = slot_of.reshape(TOP_K, T // tm, tm).transpose(1, 0, 2)
    const = lambda shape: pl.BlockSpec(shape, lambda i: (0, 0))
    return pl.pallas_call(
        functools.partial(_final_kernel, tm=tm),
        grid=(T // tm,),
        in_specs=[pl.BlockSpec((1, TOP_K, tm), lambda i: (i, 0, 0), memory_space=pltpu.SMEM),
                  pl.BlockSpec(memory_space=pl.ANY),
                  pl.BlockSpec((tm, D), lambda i: (i, 0)),
                  pl.BlockSpec((tm, P), lambda i: (i, 0)),
                  const((1, D)), const((D, D)), const((P, D)), const((1, D))],
        out_specs=pl.BlockSpec((tm, D), lambda i: (i, 0)),
        out_shape=jax.ShapeDtypeStruct((T, D), F32),
        scratch_shapes=[pltpu.VMEM((TOP_K, tm, D), F32),
                        pltpu.SemaphoreType.DMA(())],
        compiler_params=_params("arbitrary"),
        name="combine_ple_final",
    )(slots, y_slots, x1, p, g_ple.reshape(1, D), w_gate, w_proj, g_final.reshape(1, D))


def kernel(x_prompt, x_sample, p_prompt, p_sample, g_mix, w_in, q_gain, k_gain, conv_w, conv_b, filt_w1, filt_b1, filt_w2, filt_b2, filt_w3, filt_b3, filt_freq, filt_w_out, filt_decay, hyena_bias, g_attn_out, g_hyena_out, w_out, g_ffn, w_router, b_router, w_up, b_up, w_down, b_down, g_ple, w_ple_gate, w_ple_proj, g_final):
    depth = g_mix.shape[0]
    bp, L, D = x_prompt.shape
    bs = x_sample.shape[0]
    B = bp + bs
    T = B * L
    C = hyena_bias.shape[1]
    x = jnp.concatenate([x_prompt, x_sample], axis=0).reshape(T, D)
    p = jnp.concatenate([p_prompt, p_sample], axis=1)
    cos_t, sin_t = rope_tables(L)
    n_pair = bp if bp == 2 else 0

    for i in range(depth):
        proj = norm_matmul(x, g_mix[i], w_in[i].astype(BF16))
        q, k, v = qkv_prep(proj, q_gain[i], k_gain[i], cos_t, sin_t, L)
        attn = flash_attention(q.reshape(B, L, ATTN_WIDTH), k.reshape(B, L, KV_WIDTH),
                               v.reshape(B, L, 2 * KV_WIDTH)).reshape(T, ATTN_WIDTH)
        u, x0 = hyena_pre(proj, conv_w[i], conv_b[i], L)
        kern = long_conv_kernel_rows(L, filt_w1[i], filt_b1[i], filt_w2[i], filt_b2[i], filt_w3[i], filt_b3[i],
                                     filt_freq[i], filt_w_out[i], filt_decay[i])
        hy = long_conv_gate(u.reshape(B, L, C), x0.reshape(B, L, C), kern, hyena_bias[i], n_pair).reshape(T, C)
        x1, h, top_idx, top_gate = post_mixer(x, attn, hy, g_attn_out[i], g_hyena_out[i],
                                              w_out[i].astype(BF16), g_ffn[i], w_router[i].T, b_router[i])
        slot_tok, slot_gate, slot_of, block_expert, n_used = route(top_idx, top_gate, EXPERT_TILE)
        x_slots = gather_rows(h, slot_tok)
        y_slots = expert_mlp(x_slots, block_expert, n_used, w_up[i].astype(BF16), b_up[i],
                             w_down[i].astype(BF16), b_down[i], slot_gate)
        x = combine_ple_final(y_slots, slot_of, x1, p[i].reshape(T, -1), g_ple[i],
                              w_ple_gate[i].astype(BF16), w_ple_proj[i].astype(BF16), g_final)
    assert depth == 1
    y = x.reshape(B, L, D)
    return (y[:bp], y[bp:])
```

```python
import functools
import math

import numpy as np
import jax
import jax.numpy as jnp
from jax import lax
from jax.experimental import pallas as pl
from jax.experimental.pallas import tpu as pltpu

F32 = jnp.float32
BF16 = jnp.bfloat16

HEAD_DIM = 128
N_Q_HEADS = 8
N_KV_HEADS = 2
Q_PER_KV = N_Q_HEADS // N_KV_HEADS
ATTN_WIDTH = N_Q_HEADS * HEAD_DIM
KV_WIDTH = N_KV_HEADS * HEAD_DIM
GRID_W = 64
ROPE_AXIS_DIM = HEAD_DIM // 2
ROPE_THETA = 10000.0
SHORT_CONV = 3
FILTER_EMB = 33
FILTER_BANDS = (FILTER_EMB - 1) // 2
N_EXPERTS = 32
TOP_K = 4
SWIGLU_ALPHA = 1.702
SWIGLU_LIMIT = 7.0
EPS = 1e-6
LOG2E = 1.4426950408889634

LANES = 128
SUBLANES = 8
VMEM_LIMIT_BYTES = 56 * 1024 * 1024

FFT_N2 = 128
EXPERT_TILE = 1024
FF_TILE = 512


def _params(*sem):
    return pltpu.CompilerParams(dimension_semantics=sem, vmem_limit_bytes=VMEM_LIMIT_BYTES)


def _pick(n, want):
    t = min(n, want)
    while n % t:
        t //= 2
    return t


def _rms(x, g):
    return x * lax.rsqrt(jnp.mean(x * x, axis=-1, keepdims=True) + EPS) * g


def _norm_matmul_kernel(x_ref, g_ref, w_ref, o_ref, h_sc):
    @pl.when(pl.program_id(1) == 0)
    def _():
        h_sc[...] = _rms(x_ref[...], g_ref[...]).astype(BF16)

    o_ref[...] = jnp.dot(h_sc[...], w_ref[...], preferred_element_type=F32)


def norm_matmul(x, g, w, tm=1024, tn=512):
    T, D = x.shape
    N = w.shape[1]
    tm, tn = _pick(T, tm), _pick(N, tn)
    return pl.pallas_call(
        _norm_matmul_kernel,
        grid=(T // tm, N // tn),
        in_specs=[pl.BlockSpec((tm, D), lambda i, j: (i, 0)),
                  pl.BlockSpec((1, D), lambda i, j: (0, 0)),
                  pl.BlockSpec((D, tn), lambda i, j: (0, j))],
        out_specs=pl.BlockSpec((tm, tn), lambda i, j: (i, j)),
        out_shape=jax.ShapeDtypeStruct((T, N), F32),
        scratch_shapes=[pltpu.VMEM((tm, D), BF16)],
        compiler_params=_params("parallel", "arbitrary"),
        name="norm_matmul",
    )(x, g.reshape(1, D), w)


def _qkv_prep_kernel(p_ref, qg_ref, kg_ref, c_ref, s_ref, q_ref, k_ref, v_ref):
    cos = c_ref[...]
    sin = s_ref[...]
    lane = lax.broadcasted_iota(jnp.int32, cos.shape, 1)
    low_half = (lane % ROPE_AXIS_DIM) < (ROPE_AXIS_DIM // 2)

    def norm_rope(x, g):
        y = _rms(x, g)
        fwd = pltpu.roll(y, ROPE_AXIS_DIM // 2, axis=1)
        bwd = pltpu.roll(y, HEAD_DIM - ROPE_AXIS_DIM // 2, axis=1)
        return y * cos + jnp.where(low_half, bwd, fwd) * sin

    q_scale = (HEAD_DIM ** -0.5) * LOG2E
    for h in range(N_Q_HEADS):
        sl = slice(h * HEAD_DIM, (h + 1) * HEAD_DIM)
        q_ref[:, sl] = (norm_rope(p_ref[:, sl], qg_ref[...]) * q_scale).astype(BF16)
    for h in range(N_KV_HEADS):
        src = slice(ATTN_WIDTH + h * HEAD_DIM, ATTN_WIDTH + (h + 1) * HEAD_DIM)
        dst = slice(h * HEAD_DIM, (h + 1) * HEAD_DIM)
        k_ref[:, dst] = norm_rope(p_ref[:, src], kg_ref[...]).astype(BF16)
    ones = jnp.ones((p_ref.shape[0], HEAD_DIM), BF16)
    for h in range(N_KV_HEADS):
        src = slice(ATTN_WIDTH + KV_WIDTH + h * HEAD_DIM, ATTN_WIDTH + KV_WIDTH + (h + 1) * HEAD_DIM)
        v_ref[:, 2 * h * HEAD_DIM:(2 * h + 1) * HEAD_DIM] = p_ref[:, src].astype(BF16)
        v_ref[:, (2 * h + 1) * HEAD_DIM:(2 * h + 2) * HEAD_DIM] = ones


def qkv_prep(proj, q_gain, k_gain, cos_t, sin_t, L, tl=512):
    T = proj.shape[0]
    tl = _pick(L, tl)
    nl = L // tl
    w_qkv = ATTN_WIDTH + 2 * KV_WIDTH
    return pl.pallas_call(
        _qkv_prep_kernel,
        grid=(T // tl,),
        in_specs=[pl.BlockSpec((tl, w_qkv), lambda i: (i, 0)),
                  pl.BlockSpec((1, HEAD_DIM), lambda i: (0, 0)),
                  pl.BlockSpec((1, HEAD_DIM), lambda i: (0, 0)),
                  pl.BlockSpec((tl, HEAD_DIM), lambda i: (i % nl, 0)),
                  pl.BlockSpec((tl, HEAD_DIM), lambda i: (i % nl, 0))],
        out_specs=[pl.BlockSpec((tl, ATTN_WIDTH), lambda i: (i, 0)),
                   pl.BlockSpec((tl, KV_WIDTH), lambda i: (i, 0)),
                   pl.BlockSpec((tl, 2 * KV_WIDTH), lambda i: (i, 0))],
        out_shape=[jax.ShapeDtypeStruct((T, ATTN_WIDTH), BF16),
                   jax.ShapeDtypeStruct((T, KV_WIDTH), BF16),
                   jax.ShapeDtypeStruct((T, 2 * KV_WIDTH), BF16)],
        compiler_params=_params("parallel"),
        name="qkv_prep",
    )(proj, q_gain.reshape(1, HEAD_DIM), k_gain.reshape(1, HEAD_DIM), cos_t, sin_t)


def rope_tables(L):
    rows = L // GRID_W
    row_idx = jnp.repeat(jnp.arange(rows, dtype=F32), GRID_W)
    col_idx = jnp.tile(jnp.arange(GRID_W, dtype=F32), rows)
    inv_freq = ROPE_THETA ** (-jnp.arange(0, ROPE_AXIS_DIM, 2, dtype=F32) / ROPE_AXIS_DIM)
    ar = row_idx[:, None] * inv_freq[None, :]
    ac = col_idx[:, None] * inv_freq[None, :]
    cos_t = jnp.concatenate([jnp.cos(ar), jnp.cos(ar), jnp.cos(ac), jnp.cos(ac)], axis=-1)
    sin_t = jnp.concatenate([-jnp.sin(ar), jnp.sin(ar), -jnp.sin(ac), jnp.sin(ac)], axis=-1)
    return cos_t, sin_t


def _attn_kernel(q_ref, k_ref, v_ref, o_ref, m_sc, acc_sc, *, tk, n_kv):
    m_sc[...] = jnp.full(m_sc.shape, -jnp.inf, F32)
    acc_sc[...] = jnp.zeros(acc_sc.shape, F32)

    def body(j, carry):
        off = pl.multiple_of(j * tk, tk)
        k = k_ref[0, pl.ds(off, tk), :]
        v = v_ref[0, pl.ds(off, tk), :]
        for g in range(Q_PER_KV):
            q = q_ref[0, :, g * HEAD_DIM:(g + 1) * HEAD_DIM]
            s = lax.dot_general(q, k, (((1,), (1,)), ((), ())), preferred_element_type=F32)
            m_prev = m_sc[g]
            m_new = jnp.maximum(m_prev, jnp.max(s, axis=-1, keepdims=True))
            alpha = jnp.exp2(m_prev - m_new)
            p = jnp.exp2(s - jnp.tile(m_new, (1, tk // LANES)))
            pv = jnp.dot(p.astype(BF16), v, preferred_element_type=F32)
            acc_sc[g] = jnp.tile(alpha, (1, 2 * HEAD_DIM // LANES)) * acc_sc[g] + pv
            m_sc[g] = m_new
        return carry

    lax.fori_loop(0, n_kv, body, 0)
    for g in range(Q_PER_KV):
        acc = acc_sc[g]
        o_ref[0, :, g * HEAD_DIM:(g + 1) * HEAD_DIM] = acc[:, :HEAD_DIM] / acc[:, HEAD_DIM:]


def flash_attention(q, k, v1, tq=1024, tk=512):
    B, L, _ = q.shape
    tq, tk = _pick(L, tq), _pick(L, tk)
    gw = Q_PER_KV * HEAD_DIM
    kern = functools.partial(_attn_kernel, tk=tk, n_kv=L // tk)
    return pl.pallas_call(
        kern,
        grid=(B, N_KV_HEADS, L // tq),
        in_specs=[pl.BlockSpec((1, tq, gw), lambda b, h, i: (b, i, h)),
                  pl.BlockSpec((1, L, HEAD_DIM), lambda b, h, i: (b, 0, h)),
                  pl.BlockSpec((1, L, 2 * HEAD_DIM), lambda b, h, i: (b, 0, h))],
        out_specs=pl.BlockSpec((1, tq, gw), lambda b, h, i: (b, i, h)),
        out_shape=jax.ShapeDtypeStruct((B, L, ATTN_WIDTH), F32),
        scratch_shapes=[pltpu.VMEM((Q_PER_KV, tq, LANES), F32),
                        pltpu.VMEM((Q_PER_KV, tq, 2 * HEAD_DIM), F32)],
        compiler_params=_params("parallel", "parallel", "arbitrary"),
        name="flash_attention",
    )(q, k, v1)


def _hyena_pre_kernel(x0_ref, x1_ref, v_ref, x0p_ref, x1p_ref, vp_ref, x0n_ref, x1n_ref, vn_ref,
                      w0_ref, w1_ref, w2_ref, b0_ref, b1_ref, b2_ref, u_ref, x0o_ref, *, n_l):
    i = pl.program_id(0) % n_l
    first = i == 0
    last = i == n_l - 1
    tl = x0_ref.shape[0]
    row = lax.broadcasted_iota(jnp.int32, x0_ref.shape, 0)

    def conv(x_ref, p_ref, n_ref, w_ref, b_ref):
        x = x_ref[...]
        prev_row = jnp.where(first, 0.0, p_ref[SUBLANES - 1:SUBLANES, :])
        next_row = jnp.where(last, 0.0, n_ref[0:1, :])
        xm = jnp.where(row == 0, prev_row, pltpu.roll(x, 1, axis=0))
        xp = jnp.where(row == tl - 1, next_row, pltpu.roll(x, tl - 1, axis=0))
        return xm * w_ref[0:1, :] + x * w_ref[1:2, :] + xp * w_ref[2:3, :] + b_ref[...]

    x0o_ref[...] = conv(x0_ref, x0p_ref, x0n_ref, w0_ref, b0_ref)
    u_ref[...] = conv(v_ref, vp_ref, vn_ref, w2_ref, b2_ref) * conv(x1_ref, x1p_ref, x1n_ref, w1_ref, b1_ref)


def hyena_pre(proj, conv_w, conv_b, L, tl=512, tc=512):
    T = proj.shape[0]
    C = conv_w.shape[1] // 3
    tl, tc = _pick(L, tl), _pick(C, tc)
    n_l = L // tl
    nc = C // tc
    base = (ATTN_WIDTH + 2 * KV_WIDTH) // tc
    r8 = tl // SUBLANES
    n8 = T // SUBLANES

    def main(k):
        return pl.BlockSpec((tl, tc), lambda i, j: (i, base + k * nc + j))

    def prev(k):
        return pl.BlockSpec((SUBLANES, tc), lambda i, j: (jnp.maximum(i * r8 - 1, 0), base + k * nc + j))

    def nxt(k):
        return pl.BlockSpec((SUBLANES, tc), lambda i, j: (jnp.minimum((i + 1) * r8, n8 - 1), base + k * nc + j))

    def wsp(k):
        return pl.BlockSpec((SHORT_CONV, tc), lambda i, j: (0, k * nc + j))

    def bsp(k):
        return pl.BlockSpec((1, tc), lambda i, j: (0, k * nc + j))

    cb = conv_b.reshape(1, 3 * C)
    kern = functools.partial(_hyena_pre_kernel, n_l=n_l)
    return pl.pallas_call(
        kern,
        grid=(T // tl, nc),
        in_specs=[main(0), main(1), main(2), prev(0), prev(1), prev(2), nxt(0), nxt(1), nxt(2),
                  wsp(0), wsp(1), wsp(2), bsp(0), bsp(1), bsp(2)],
        out_specs=[pl.BlockSpec((tl, tc), lambda i, j: (i, j)),
                   pl.BlockSpec((tl, tc), lambda i, j: (i, j))],
        out_shape=[jax.ShapeDtypeStruct((T, C), F32), jax.ShapeDtypeStruct((T, C), F32)],
        compiler_params=_params("parallel", "parallel"),
        name="hyena_pre",
    )(proj, proj, proj, proj, proj, proj, proj, proj, proj,
      conv_w, conv_w, conv_w, cb, cb, cb)


def _filter_kernel(z_ref, t_ref, w1_ref, b1_ref, w2_ref, b2_ref, w3_ref, b3_ref, fr_ref,
                   wo_ref, dec_ref, wob_ref, decb_ref, o_ref, *, n_half):
    i = pl.program_id(0)
    hi = lax.Precision.HIGHEST
    fr = fr_ref[...]
    h = jnp.sin(fr * (jnp.dot(z_ref[...], w1_ref[...], precision=hi, preferred_element_type=F32) + b1_ref[...]))
    h = jnp.sin(fr * (jnp.dot(h, w2_ref[...], precision=hi, preferred_element_type=F32) + b2_ref[...]))
    h = jnp.sin(fr * (jnp.dot(h, w3_ref[...], precision=hi, preferred_element_type=F32) + b3_ref[...]))
    taps = jnp.dot(h, wo_ref[...], precision=hi, preferred_element_type=F32)
    o_ref[...] = taps * jnp.exp(-t_ref[...] * jnp.abs(dec_ref[...]))
    first_row = lax.broadcasted_iota(jnp.int32, (SUBLANES, o_ref.shape[1]), 0) == 0

    @pl.when(i == 0)
    def _():
        hb = jnp.dot(h[:SUBLANES], wob_ref[...], precision=hi, preferred_element_type=F32)
        hb = hb * jnp.exp(-t_ref[0:SUBLANES, :] * jnp.abs(decb_ref[...]))
        o_ref[0:SUBLANES, :] = o_ref[0:SUBLANES, :] + jnp.where(first_row, hb, 0.0)

    @pl.when(i == n_half)
    def _():
        o_ref[0:SUBLANES, :] = jnp.where(first_row, 0.0, o_ref[0:SUBLANES, :])


def filter_features(L):
    t = jnp.linspace(0.0, 1.0, L, dtype=F32)[:, None]
    w = 2.0 * math.pi * jnp.arange(L, dtype=F32) / L
    bands = jnp.linspace(1e-4, FILTER_BANDS - 1, FILTER_BANDS, dtype=F32)
    ang = w[:, None] * bands[None, :]
    z = jnp.concatenate([t, jnp.cos(ang), -jnp.sin(ang)], axis=-1)
    return z, t


def long_conv_kernel_rows(L, w1, b1, w2, b2, w3, b3, freq, w_out, decay, tl=512, tn=1024):
    z, t = filter_features(L)
    mirror = lambda a: jnp.concatenate([a, a[:1], a[:0:-1]], axis=0)
    z, t = mirror(z), mirror(t)
    emb, hid = w1.shape
    emb_pad = -(-emb // LANES) * LANES
    z = jnp.pad(z, ((0, 0), (0, emb_pad - emb)))
    w1 = jnp.pad(w1, ((0, emb_pad - emb), (0, 0)))
    C = w_out.shape[1] // 2
    tl, tn = _pick(L, tl), _pick(C, tn)
    n_half, n_c = L // tl, C // tn
    row = lambda a: a.reshape(1, -1)
    const = lambda shape: pl.BlockSpec(shape, lambda i, j: (0, 0))
    side = lambda i, j: (0, jnp.where(i < n_half, j, n_c + j))
    back = lambda i, j: (0, n_c + j)
    return pl.pallas_call(
        functools.partial(_filter_kernel, n_half=n_half),
        grid=(2 * n_half, n_c),
        in_specs=[pl.BlockSpec((tl, emb_pad), lambda i, j: (i, 0)),
                  pl.BlockSpec((tl, 1), lambda i, j: (i, 0)),
                  const((emb_pad, hid)), const((1, hid)),
                  const((hid, hid)), const((1, hid)),
                  const((hid, hid)), const((1, hid)),
                  const((1, hid)),
                  pl.BlockSpec((hid, tn), side), pl.BlockSpec((1, tn), side),
                  pl.BlockSpec((hid, tn), back), pl.BlockSpec((1, tn), back)],
        out_specs=pl.BlockSpec((tl, tn), lambda i, j: (i, j)),
        out_shape=jax.ShapeDtypeStruct((2 * L, C), F32),
        compiler_params=_params("parallel", "parallel"),
        name="implicit_filter",
    )(z, t, w1, row(b1), w2, row(b2), w3, row(b3), row(freq), w_out, row(decay), w_out, row(decay))


def _dft_consts(L):
    N = 2 * L
    N2 = FFT_N2
    N1 = N // N2
    H = N1 // 2
    k1 = np.arange(N1)
    f1 = np.exp(-2j * np.pi * np.outer(k1, k1) / N1)
    fh = f1[:, :H]
    m1_pair = np.block([[fh.real, -fh.imag], [fh.imag, fh.real]])
    m1_single = np.concatenate([fh.real, fh.imag], axis=0)
    m1_full = np.concatenate([f1.real, f1.imag], axis=0)
    n2 = np.arange(N2)
    f2 = np.exp(-2j * np.pi * np.outer(n2, n2) / N2)
    tw = np.exp(-2j * np.pi * np.outer(k1, n2) / N)
    c = lambda a: jnp.asarray(a, F32)
    return dict(N=N, N1=N1, N2=N2, H=H,
                m1_pair=c(m1_pair), m1_single=c(m1_single), m1_full=c(m1_full),
                m3_pair=c(m1_pair.T / N), m3_single=c(m1_single.T / N),
                f2r=c(f2.real), f2i=c(f2.imag),
                twr=c(tw.real).reshape(N1, 1, N2), twi=c(tw.imag).reshape(N1, 1, N2))


def _dft_stage1_kernel(m_ref, x_ref, o_ref):
    rows = m_ref.shape[1]
    tc = x_ref.shape[-1]
    m = m_ref[...].astype(BF16)
    x = pltpu.einshape("mnc->nmc", x_ref[...].reshape(rows, SUBLANES, tc))
    r = jnp.stack([jnp.dot(m, x[s].astype(BF16), preferred_element_type=F32) for s in range(SUBLANES)], axis=0)
    o_ref[...] = pltpu.einshape("nmc->mnc", r).reshape(o_ref.shape)


def dft_stage1(x4, m1, tc=256):
    S, R, N2, C = x4.shape
    N1 = m1.shape[0] // 2
    tc = _pick(C, tc)
    return pl.pallas_call(
        _dft_stage1_kernel,
        grid=(N2 // SUBLANES, C // tc),
        in_specs=[pl.BlockSpec(m1.shape, lambda j, c: (0, 0)),
                  pl.BlockSpec((S, R, SUBLANES, tc), lambda j, c: (0, 0, j, c))],
        out_specs=pl.BlockSpec((2, N1, SUBLANES, tc), lambda j, c: (0, 0, j, c)),
        out_shape=jax.ShapeDtypeStruct((2, N1, N2, C), F32),
        compiler_params=_params("parallel", "parallel"),
        name="dft_stage1",
    )(m1, x4)


def _stage2_matrix(f2r_ref, f2i_ref, twr_ref, twi_ref):
    f2r, f2i = f2r_ref[...], f2i_ref[...]
    twr, twi = twr_ref[...], twi_ref[...]
    er = f2r * twr - f2i * twi
    ei = f2r * twi + f2i * twr
    return jnp.concatenate([jnp.concatenate([er, -ei], axis=1),
                            jnp.concatenate([ei, er], axis=1)], axis=0)


def _dft_stage2_kernel(f2r_ref, f2i_ref, twr_ref, twi_ref, a_ref, o_ref):
    n2, c = a_ref.shape[1], a_ref.shape[2]
    m = _stage2_matrix(f2r_ref, f2i_ref, twr_ref, twi_ref).astype(BF16)
    a = a_ref[...].reshape(2 * n2, c).astype(BF16)
    o_ref[...] = jnp.dot(m, a, preferred_element_type=F32).reshape(o_ref.shape)


def _spec_blocks(N1, N2, C):
    mat = pl.BlockSpec((N2, N2), lambda i: (0, 0))
    tw = pl.BlockSpec((None, 1, N2), lambda i: (i, 0, 0))
    dat = pl.BlockSpec((2, None, N2, C), lambda i: (0, i, 0, 0))
    return mat, tw, dat


def dft_stage2(a, cst):
    _, N1, N2, C = a.shape
    mat, tw, dat = _spec_blocks(N1, N2, C)
    return pl.pallas_call(
        _dft_stage2_kernel,
        grid=(N1,),
        in_specs=[mat, mat, tw, tw, dat],
        out_specs=dat,
        out_shape=jax.ShapeDtypeStruct(a.shape, F32),
        compiler_params=_params("parallel"),
        name="dft_stage2",
    )(cst["f2r"], cst["f2i"], cst["twr"], cst["twi"], a)


def _spectral_kernel(f2r_ref, f2i_ref, twr_ref, twi_ref, a_ref, k_ref, o_ref):
    n2, c = a_ref.shape[1], a_ref.shape[2]
    m = _stage2_matrix(f2r_ref, f2i_ref, twr_ref, twi_ref)
    a = a_ref[...].reshape(2 * n2, c).astype(BF16)
    x = jnp.dot(m.astype(BF16), a, preferred_element_type=F32)
    xr, xi = x[:n2], x[n2:]
    kr, ki = k_ref[0], k_ref[1]
    y = jnp.concatenate([xr * kr - xi * ki, xr * ki + xi * kr], axis=0).astype(BF16)
    o_ref[...] = jnp.dot(m.T.astype(BF16), y, preferred_element_type=F32).reshape(o_ref.shape)


def spectral_multiply(a, kspec, cst):
    _, N1, N2, C = a.shape
    mat, tw, dat = _spec_blocks(N1, N2, C)
    return pl.pallas_call(
        _spectral_kernel,
        grid=(N1,),
        in_specs=[mat, mat, tw, tw, dat, dat],
        out_specs=dat,
        out_shape=jax.ShapeDtypeStruct(a.shape, F32),
        compiler_params=_params("parallel"),
        name="spectral_multiply",
    )(cst["f2r"], cst["f2i"], cst["twr"], cst["twi"], a, kspec)


def _dft_final_kernel(m_ref, b_ref, u_ref, x0_ref, bias_ref, o_ref):
    rows = m_ref.shape[1]
    c = b_ref.shape[-1]
    m = m_ref[...].astype(BF16)
    b = pltpu.einshape("mnc->nmc", b_ref[...].reshape(rows, SUBLANES, c))
    y = jnp.stack([jnp.dot(m, b[s].astype(BF16), preferred_element_type=F32) for s in range(SUBLANES)], axis=0)
    y = pltpu.einshape("nmc->mnc", y).reshape(o_ref.shape)
    o_ref[...] = (y + u_ref[...] * bias_ref[...]) * x0_ref[...]


def dft_final(b, m3, u4, x04, bias, tc=256):
    S, R, N2, C = u4.shape
    N1 = b.shape[1]
    tc = _pick(C, tc)
    seq = pl.BlockSpec((S, R, SUBLANES, tc), lambda j, c: (0, 0, j, c))
    return pl.pallas_call(
        _dft_final_kernel,
        grid=(N2 // SUBLANES, C // tc),
        in_specs=[pl.BlockSpec(m3.shape, lambda j, c: (0, 0)),
                  pl.BlockSpec((2, N1, SUBLANES, tc), lambda j, c: (0, 0, j, c)),
                  seq, seq,
                  pl.BlockSpec((1, tc), lambda j, c: (0, c))],
        out_specs=seq,
        out_shape=jax.ShapeDtypeStruct(u4.shape, F32),
        compiler_params=_params("parallel", "parallel"),
        name="dft_final",
    )(m3, b, u4, x04, bias.reshape(1, C))


def long_conv_gate(u, x0, kern, bias, n_pair):
    B, L, C = u.shape
    cst = _dft_consts(L)
    N1, N2, H = cst["N1"], cst["N2"], cst["H"]
    kspec = dft_stage2(dft_stage1(kern.reshape(1, N1, N2, C), cst["m1_full"]), cst)
    outs = []
    for lo, hi in ((0, n_pair), (n_pair, B)):
        if hi == lo:
            continue
        pair = (hi - lo) == 2
        u4 = u[lo:hi].reshape(hi - lo, H, N2, C)
        x04 = x0[lo:hi].reshape(hi - lo, H, N2, C)
        a = dft_stage1(u4, cst["m1_pair"] if pair else cst["m1_single"])
        bsp = spectral_multiply(a, kspec, cst)
        y = dft_final(bsp, cst["m3_pair"] if pair else cst["m3_single"], u4, x04, bias)
        outs.append(y.reshape(hi - lo, L, C))
    return jnp.concatenate(outs, axis=0) if len(outs) > 1 else outs[0]


def _pack_bf16_pairs(h):
    half = h.shape[1] // 2
    bits = lax.bitcast_convert_type(h.astype(BF16).astype(F32), jnp.uint32)
    return bits[:, :half] | (bits[:, half:] >> 16)


def _unpack_bf16_pairs(w):
    hi = lax.bitcast_convert_type(w & jnp.uint32(0xFFFF0000), F32).astype(BF16)
    lo = lax.bitcast_convert_type(w << 16, F32).astype(BF16)
    return hi, lo


def _post_mixer_kernel(x_ref, a_ref, hy_ref, ga_ref, gh_ref, wo_ref, gf_ref, wr_ref, br_ref,
                       x1_ref, h_ref, idx_ref, gate_ref):
    merged = jnp.concatenate([_rms(a_ref[...], ga_ref[...]), _rms(hy_ref[...], gh_ref[...])], axis=-1)
    x1 = x_ref[...] + jnp.dot(merged.astype(BF16), wo_ref[...], preferred_element_type=F32)
    x1_ref[...] = x1
    h = _rms(x1, gf_ref[...])
    h_ref[...] = _pack_bf16_pairs(h)
    logits = lax.dot_general(wr_ref[...], h, (((1,), (1,)), ((), ())),
                             precision=lax.Precision.HIGHEST, preferred_element_type=F32) + br_ref[...]
    e_iota = lax.broadcasted_iota(jnp.int32, logits.shape, 0).astype(F32)
    vals, idxs = [], []
    cur = logits
    for _ in range(TOP_K):
        m = jnp.max(cur, axis=0, keepdims=True)
        sel = jnp.min(jnp.where(cur == m, e_iota, float(N_EXPERTS)), axis=0, keepdims=True)
        vals.append(m)
        idxs.append(sel)
        cur = jnp.where(e_iota == sel, -jnp.inf, cur)
    ex = [jnp.exp(v - vals[0]) for v in vals]
    den = ex[0] + ex[1] + ex[2] + ex[3]
    idx_ref[...] = jnp.concatenate(idxs, axis=0).astype(jnp.int32)
    gate_ref[...] = jnp.concatenate([e / den for e in ex], axis=0)


def post_mixer(x, attn, hy, g_attn, g_hy, w_out, g_ffn, w_router_t, b_router, tm=512):
    T, D = x.shape
    Wa, Wh = attn.shape[1], hy.shape[1]
    tm = _pick(T, tm)
    const = lambda shape: pl.BlockSpec(shape, lambda i: (0, 0))
    rows = lambda w: pl.BlockSpec((tm, w), lambda i: (i, 0))
    cols = pl.BlockSpec((TOP_K, tm), lambda i: (0, i))
    return pl.pallas_call(
        _post_mixer_kernel,
        grid=(T // tm,),
        in_specs=[rows(D), rows(Wa), rows(Wh), const((1, Wa)), const((1, Wh)), const((Wa + Wh, D)),
                  const((1, D)), const((N_EXPERTS, D)), const((N_EXPERTS, 1))],
        out_specs=[rows(D), rows(D // 2), cols, cols],
        out_shape=[jax.ShapeDtypeStruct((T, D), F32), jax.ShapeDtypeStruct((T, D // 2), jnp.uint32),
                   jax.ShapeDtypeStruct((TOP_K, T), jnp.int32), jax.ShapeDtypeStruct((TOP_K, T), F32)],
        compiler_params=_params("parallel"),
        name="post_mixer",
    )(x, attn, hy, g_attn.reshape(1, Wa), g_hy.reshape(1, Wh), w_out, g_ffn.reshape(1, D),
      w_router_t, b_router.reshape(N_EXPERTS, 1))


def route(top_idx, top_gate, tile):
    K, T = top_idx.shape
    n_assign = K * T
    e_flat = top_idx.reshape(-1)
    pos = jnp.arange(n_assign, dtype=jnp.int32)
    e_sorted, order, gate_sorted = lax.sort((e_flat, pos, top_gate.reshape(-1)), num_keys=1, is_stable=True)
    tok_sorted = order % T
    counts = jnp.sum((e_flat[None, :] == jnp.arange(N_EXPERTS, dtype=jnp.int32)[:, None]).astype(jnp.int32), axis=1)
    start = jnp.cumsum(counts) - counts
    padded = (counts + tile - 1) // tile * tile
    pend = jnp.cumsum(padded)
    pstart = pend - padded
    n_blocks = n_assign // tile + N_EXPERTS
    n_slots = n_blocks * tile
    block_start = jnp.arange(n_blocks, dtype=jnp.int32) * tile
    n_used = (pend[-1] // tile).astype(jnp.int32)
    block_expert = jnp.minimum(jnp.sum(block_start[:, None] >= pend[None, :], axis=1), N_EXPERTS - 1)
    last_expert = block_expert[jnp.maximum(n_used - 1, 0)]
    block_expert = jnp.where(jnp.arange(n_blocks) < n_used, block_expert, last_expert).astype(jnp.int32)
    slot = jnp.arange(n_slots, dtype=jnp.int32)
    slot_e = jnp.repeat(block_expert, tile)
    within = slot - pstart[slot_e]
    valid = jnp.logical_and(within < counts[slot_e], slot < pend[-1])
    src = jnp.clip(start[slot_e] + within, 0, n_assign - 1)
    slot_tok = jnp.where(valid, tok_sorted[src], 0)
    slot_gate = jnp.where(valid, gate_sorted[src], 0.0)
    slot_sorted = pstart[e_sorted] + (pos - start[e_sorted])
    _, slot_of = lax.sort((order, slot_sorted), num_keys=1)
    return slot_tok, slot_gate, slot_of.reshape(K, T), block_expert, n_used.reshape(1)


def _row_copy(src_ref, src_row, dst_ref, dst_row, sem):
    return pltpu.make_async_copy(src_ref.at[pl.ds(src_row, 1), :], dst_ref.at[pl.ds(dst_row, 1), :], sem)


DMA_ISSUE_UNROLL = 8


def _gather_kernel(nrows_ref, idx_ref, src_ref, dst_ref, sem, *, rows):
    live = pl.program_id(0) * rows < nrows_ref[0]

    @pl.when(live)
    def _():
        def issue(r, c):
            _row_copy(src_ref, idx_ref[0, 0, r], dst_ref, r, sem).start()
            return c

        lax.fori_loop(0, rows, issue, 0, unroll=DMA_ISSUE_UNROLL)
        pltpu.make_async_copy(src_ref.at[pl.ds(0, rows), :], dst_ref, sem).wait()

    @pl.when(jnp.logical_not(live))
    def _():
        dst_ref[...] = jnp.zeros(dst_ref.shape, dst_ref.dtype)


def gather_rows(src, idx, n_live_rows, rows=512):
    T, W = src.shape
    n = idx.shape[0]
    rows = _pick(n, rows)
    return pl.pallas_call(
        functools.partial(_gather_kernel, rows=rows),
        grid=(n // rows,),
        in_specs=[pl.BlockSpec(memory_space=pltpu.SMEM),
                  pl.BlockSpec((1, 1, rows), lambda i: (i, 0, 0), memory_space=pltpu.SMEM),
                  pl.BlockSpec(memory_space=pl.ANY)],
        out_specs=pl.BlockSpec((rows, W), lambda i: (i, 0)),
        out_shape=jax.ShapeDtypeStruct((n, W), src.dtype),
        scratch_shapes=[pltpu.SemaphoreType.DMA(())],
        compiler_params=_params("arbitrary"),
        name="gather_rows",
    )(n_live_rows, idx.reshape(n // rows, 1, rows), src)


def _expert_kernel(be_ref, nu_ref, x_ref, wg_ref, wl_ref, bg_ref, bl_ref, wd_ref, bd_ref, g_ref,
                   o_ref, xs_sc, *, n_f):
    b = pl.program_id(0)
    f = pl.program_id(1)
    half = x_ref.shape[1]
    used = b < nu_ref[0]

    @pl.when(jnp.logical_and(used, f == 0))
    def _():
        hi, lo = _unpack_bf16_pairs(x_ref[...])
        xs_sc[:, :half] = hi
        xs_sc[:, half:] = lo

    @pl.when(used)
    def _():
        x = xs_sc[...]
        glu = jnp.dot(x, wg_ref[0], preferred_element_type=F32) + bg_ref[0]
        lin = jnp.dot(x, wl_ref[0], preferred_element_type=F32) + bl_ref[0]
        glu = jnp.minimum(glu, SWIGLU_LIMIT)
        lin = jnp.clip(lin, -SWIGLU_LIMIT, SWIGLU_LIMIT)
        y = glu * jax.nn.sigmoid(SWIGLU_ALPHA * glu) * (lin + 1.0)
        part = jnp.dot(y.astype(BF16), wd_ref[0], preferred_element_type=F32)

        @pl.when(f == 0)
        def _():
            o_ref[...] = part

        @pl.when(f > 0)
        def _():
            o_ref[...] += part

        @pl.when(f == n_f - 1)
        def _():
            g = g_ref[...]
            for c in range(o_ref.shape[1] // LANES):
                sl = slice(c * LANES, (c + 1) * LANES)
                o_ref[:, sl] = (o_ref[:, sl] + bd_ref[0, :, sl]) * g

    @pl.when(jnp.logical_and(jnp.logical_not(used), f == n_f - 1))
    def _():
        o_ref[...] = jnp.zeros(o_ref.shape, o_ref.dtype)


def expert_mlp(x_slots, block_expert, n_used, w_up, b_up, w_down, b_down, slot_gate,
               tile=EXPERT_TILE, tf=FF_TILE):
    n_slots, half = x_slots.shape
    D = 2 * half
    E, _, two_ff = w_up.shape
    d_ff = two_ff // 2
    tf = _pick(d_ff, tf)
    n_f = d_ff // tf
    n_blocks = n_slots // tile
    gate_rep = jnp.broadcast_to(slot_gate[:, None], (n_slots, LANES))

    def blk(b, nu):
        return jnp.minimum(b, jnp.maximum(nu[0] - 1, 0))

    def ff(b, f, nu):
        return jnp.where(b < nu[0], f, n_f - 1)

    grid_spec = pltpu.PrefetchScalarGridSpec(
        num_scalar_prefetch=2,
        grid=(n_blocks, n_f),
        in_specs=[pl.BlockSpec((tile, half), lambda b, f, be, nu: (blk(b, nu), 0)),
                  pl.BlockSpec((1, D, tf), lambda b, f, be, nu: (be[b], 0, ff(b, f, nu))),
                  pl.BlockSpec((1, D, tf), lambda b, f, be, nu: (be[b], 0, n_f + ff(b, f, nu))),
                  pl.BlockSpec((1, 1, tf), lambda b, f, be, nu: (be[b], 0, ff(b, f, nu))),
                  pl.BlockSpec((1, 1, tf), lambda b, f, be, nu: (be[b], 0, n_f + ff(b, f, nu))),
                  pl.BlockSpec((1, tf, D), lambda b, f, be, nu: (be[b], ff(b, f, nu), 0)),
                  pl.BlockSpec((1, 1, D), lambda b, f, be, nu: (be[b], 0, 0)),
                  pl.BlockSpec((tile, LANES), lambda b, f, be, nu: (blk(b, nu), 0))],
        out_specs=pl.BlockSpec((tile, D), lambda b, f, be, nu: (b, 0)),
        scratch_shapes=[pltpu.VMEM((tile, D), BF16)])
    return pl.pallas_call(
        functools.partial(_expert_kernel, n_f=n_f),
        grid_spec=grid_spec,
        out_shape=jax.ShapeDtypeStruct((n_slots, D), F32),
        compiler_params=_params("arbitrary", "arbitrary"),
        name="expert_mlp",
    )(block_expert, n_used, x_slots, w_up, w_up, b_up.reshape(E, 1, two_ff), b_up.reshape(E, 1, two_ff),
      w_down, b_down.reshape(E, 1, D), gate_rep)


def _final_kernel(slot_ref, y_hbm, x1_ref, p_ref, gp_ref, wg_ref, wp_ref, gf_ref, o_ref, rows_sc, sem,
                  *, tm):
    def issue(r, c):
        for k in range(TOP_K):
            _row_copy(y_hbm, slot_ref[0, k, r], rows_sc.at[k], r, sem).start()
        return c

    lax.fori_loop(0, tm, issue, 0, unroll=DMA_ISSUE_UNROLL // TOP_K)
    for k in range(TOP_K):
        pltpu.make_async_copy(y_hbm.at[pl.ds(0, tm), :], rows_sc.at[k], sem).wait()

    x2 = x1_ref[...] + ((rows_sc[0] + rows_sc[1]) + (rows_sc[2] + rows_sc[3]))
    gate = jax.nn.sigmoid(jnp.dot(_rms(x2, gp_ref[...]).astype(BF16), wg_ref[...],
                                  preferred_element_type=F32))
    x3 = x2 + gate * jnp.dot(p_ref[...].astype(BF16), wp_ref[...], preferred_element_type=F32)
    o_ref[...] = _rms(x3, gf_ref[...])


def combine_ple_final(y_slots, slot_of, x1, p, g_ple, w_gate, w_proj, g_final, tm=256):
    T, D = x1.shape
    P = p.shape[1]
    tm = _pick(T, tm)
    slots = slot_of.reshape(TOP_K, T // tm, tm).transpose(1, 0, 2)
    const = lambda shape: pl.BlockSpec(shape, lambda i: (0, 0))
    return pl.pallas_call(
        functools.partial(_final_kernel, tm=tm),
        grid=(T // tm,),
        in_specs=[pl.BlockSpec((1, TOP_K, tm), lambda i: (i, 0, 0), memory_space=pltpu.SMEM),
                  pl.BlockSpec(memory_space=pl.ANY),
                  pl.BlockSpec((tm, D), lambda i: (i, 0)),
                  pl.BlockSpec((tm, P), lambda i: (i, 0)),
                  const((1, D)), const((D, D)), const((P, D)), const((1, D))],
        out_specs=pl.BlockSpec((tm, D), lambda i: (i, 0)),
        out_shape=jax.ShapeDtypeStruct((T, D), F32),
        scratch_shapes=[pltpu.VMEM((TOP_K, tm, D), F32),
                        pltpu.SemaphoreType.DMA(())],
        compiler_params=_params("arbitrary"),
        name="combine_ple_final",
    )(slots, y_slots, x1, p, g_ple.reshape(1, D), w_gate, w_proj, g_final.reshape(1, D))


def kernel(x_prompt, x_sample, p_prompt, p_sample, g_mix, w_in, q_gain, k_gain, conv_w, conv_b, filt_w1, filt_b1, filt_w2, filt_b2, filt_w3, filt_b3, filt_freq, filt_w_out, filt_decay, hyena_bias, g_attn_out, g_hyena_out, w_out, g_ffn, w_router, b_router, w_up, b_up, w_down, b_down, g_ple, w_ple_gate, w_ple_proj, g_final):
    depth = g_mix.shape[0]
    bp, L, D = x_prompt.shape
    bs = x_sample.shape[0]
    B = bp + bs
    T = B * L
    C = hyena_bias.shape[1]
    x = jnp.concatenate([x_prompt, x_sample], axis=0).reshape(T, D)
    p = jnp.concatenate([p_prompt, p_sample], axis=1)
    cos_t, sin_t = rope_tables(L)
    n_pair = bp if bp == 2 else 0

    for i in range(depth):
        proj = norm_matmul(x, g_mix[i], w_in[i].astype(BF16))
        q, k, v = qkv_prep(proj, q_gain[i], k_gain[i], cos_t, sin_t, L)
        attn = flash_attention(q.reshape(B, L, ATTN_WIDTH), k.reshape(B, L, KV_WIDTH),
                               v.reshape(B, L, 2 * KV_WIDTH)).reshape(T, ATTN_WIDTH)
        u, x0 = hyena_pre(proj, conv_w[i], conv_b[i], L)
        kern = long_conv_kernel_rows(L, filt_w1[i], filt_b1[i], filt_w2[i], filt_b2[i], filt_w3[i], filt_b3[i],
                                     filt_freq[i], filt_w_out[i], filt_decay[i])
        hy = long_conv_gate(u.reshape(B, L, C), x0.reshape(B, L, C), kern, hyena_bias[i], n_pair).reshape(T, C)
        x1, h, top_idx, top_gate = post_mixer(x, attn, hy, g_attn_out[i], g_hyena_out[i],
                                              w_out[i].astype(BF16), g_ffn[i], w_router[i].T, b_router[i])
        slot_tok, slot_gate, slot_of, block_expert, n_used = route(top_idx, top_gate, EXPERT_TILE)
        x_slots = gather_rows(h, slot_tok, n_used * EXPERT_TILE)
        y_slots = expert_mlp(x_slots, block_expert, n_used, w_up[i].astype(BF16), b_up[i],
                             w_down[i].astype(BF16), b_down[i], slot_gate)
        x = combine_ple_final(y_slots, slot_of, x1, p[i].reshape(T, -1), g_ple[i],
                              w_ple_gate[i].astype(BF16), w_ple_proj[i].astype(BF16), g_final)
    assert depth == 1
    y = x.reshape(B, L, D)
    return (y[:bp], y[bp:])
```

```python
import functools
import math

import numpy as np
import jax
import jax.numpy as jnp
from jax import lax
from jax.experimental import pallas as pl
from jax.experimental.pallas import tpu as pltpu

F32 = jnp.float32
BF16 = jnp.bfloat16

HEAD_DIM = 128
N_Q_HEADS = 8
N_KV_HEADS = 2
Q_PER_KV = N_Q_HEADS // N_KV_HEADS
ATTN_WIDTH = N_Q_HEADS * HEAD_DIM
KV_WIDTH = N_KV_HEADS * HEAD_DIM
GRID_W = 64
ROPE_AXIS_DIM = HEAD_DIM // 2
ROPE_THETA = 10000.0
SHORT_CONV = 3
FILTER_EMB = 33
FILTER_BANDS = (FILTER_EMB - 1) // 2
N_EXPERTS = 32
TOP_K = 4
SWIGLU_ALPHA = 1.702
SWIGLU_LIMIT = 7.0
EPS = 1e-6
LOG2E = 1.4426950408889634

LANES = 128
SUBLANES = 8
VMEM_LIMIT_BYTES = 56 * 1024 * 1024

FFT_N2 = 128
EXPERT_TILE = 1024
FF_TILE = 512


def _params(*sem):
    return pltpu.CompilerParams(dimension_semantics=sem, vmem_limit_bytes=VMEM_LIMIT_BYTES)


def _pick(n, want):
    t = min(n, want)
    while n % t:
        t //= 2
    return t


def _rms(x, g):
    return x * lax.rsqrt(jnp.mean(x * x, axis=-1, keepdims=True) + EPS) * g


def _row_tiles_of_two(tm, width, n_a):
    a = pl.BlockSpec((tm, width), lambda i, *_: (jnp.minimum(i, n_a - 1), 0))
    b = pl.BlockSpec((tm, width), lambda i, *_: (jnp.maximum(i - n_a, 0), 0))
    return a, b


def _norm_matmul_kernel(xa_ref, xb_ref, g_ref, w_ref, o_ref, h_sc, *, n_a):
    @pl.when(pl.program_id(1) == 0)
    def _():
        x = jnp.where(pl.program_id(0) < n_a, xa_ref[...], xb_ref[...])
        h_sc[...] = _rms(x, g_ref[...]).astype(BF16)

    o_ref[...] = jnp.dot(h_sc[...], w_ref[...], preferred_element_type=F32)


def norm_matmul(xa, xb, g, w, tm=1024, tn=512):
    Ta, D = xa.shape
    Tb = xb.shape[0]
    N = w.shape[1]
    tm, tn = _pick(math.gcd(Ta, Tb), tm), _pick(N, tn)
    n_a = Ta // tm
    spec_a, spec_b = _row_tiles_of_two(tm, D, n_a)
    return pl.pallas_call(
        functools.partial(_norm_matmul_kernel, n_a=n_a),
        grid=((Ta + Tb) // tm, N // tn),
        in_specs=[spec_a, spec_b,
                  pl.BlockSpec((1, D), lambda i, j: (0, 0)),
                  pl.BlockSpec((D, tn), lambda i, j: (0, j))],
        out_specs=pl.BlockSpec((tm, tn), lambda i, j: (i, j)),
        out_shape=jax.ShapeDtypeStruct((Ta + Tb, N), F32),
        scratch_shapes=[pltpu.VMEM((tm, D), BF16)],
        compiler_params=_params("parallel", "arbitrary"),
        name="norm_matmul",
    )(xa, xb, g.reshape(1, D), w)


def _qkv_prep_kernel(p_ref, qg_ref, kg_ref, c_ref, s_ref, q_ref, k_ref, v_ref):
    cos = c_ref[...]
    sin = s_ref[...]
    lane = lax.broadcasted_iota(jnp.int32, cos.shape, 1)
    low_half = (lane % ROPE_AXIS_DIM) < (ROPE_AXIS_DIM // 2)

    def norm_rope(x, g):
        y = _rms(x, g)
        fwd = pltpu.roll(y, ROPE_AXIS_DIM // 2, axis=1)
        bwd = pltpu.roll(y, HEAD_DIM - ROPE_AXIS_DIM // 2, axis=1)
        return y * cos + jnp.where(low_half, bwd, fwd) * sin

    q_scale = (HEAD_DIM ** -0.5) * LOG2E
    for h in range(N_Q_HEADS):
        sl = slice(h * HEAD_DIM, (h + 1) * HEAD_DIM)
        q_ref[:, sl] = (norm_rope(p_ref[:, sl], qg_ref[...]) * q_scale).astype(BF16)
    for h in range(N_KV_HEADS):
        src = slice(ATTN_WIDTH + h * HEAD_DIM, ATTN_WIDTH + (h + 1) * HEAD_DIM)
        dst = slice(h * HEAD_DIM, (h + 1) * HEAD_DIM)
        k_ref[:, dst] = norm_rope(p_ref[:, src], kg_ref[...]).astype(BF16)
    ones = jnp.ones((p_ref.shape[0], HEAD_DIM), BF16)
    for h in range(N_KV_HEADS):
        src = slice(ATTN_WIDTH + KV_WIDTH + h * HEAD_DIM, ATTN_WIDTH + KV_WIDTH + (h + 1) * HEAD_DIM)
        v_ref[:, 2 * h * HEAD_DIM:(2 * h + 1) * HEAD_DIM] = p_ref[:, src].astype(BF16)
        v_ref[:, (2 * h + 1) * HEAD_DIM:(2 * h + 2) * HEAD_DIM] = ones


def qkv_prep(proj, q_gain, k_gain, cos_t, sin_t, L, tl=512):
    T = proj.shape[0]
    tl = _pick(L, tl)
    nl = L // tl
    w_qkv = ATTN_WIDTH + 2 * KV_WIDTH
    return pl.pallas_call(
        _qkv_prep_kernel,
        grid=(T // tl,),
        in_specs=[pl.BlockSpec((tl, w_qkv), lambda i: (i, 0)),
                  pl.BlockSpec((1, HEAD_DIM), lambda i: (0, 0)),
                  pl.BlockSpec((1, HEAD_DIM), lambda i: (0, 0)),
                  pl.BlockSpec((tl, HEAD_DIM), lambda i: (i % nl, 0)),
                  pl.BlockSpec((tl, HEAD_DIM), lambda i: (i % nl, 0))],
        out_specs=[pl.BlockSpec((tl, ATTN_WIDTH), lambda i: (i, 0)),
                   pl.BlockSpec((tl, KV_WIDTH), lambda i: (i, 0)),
                   pl.BlockSpec((tl, 2 * KV_WIDTH), lambda i: (i, 0))],
        out_shape=[jax.ShapeDtypeStruct((T, ATTN_WIDTH), BF16),
                   jax.ShapeDtypeStruct((T, KV_WIDTH), BF16),
                   jax.ShapeDtypeStruct((T, 2 * KV_WIDTH), BF16)],
        compiler_params=_params("parallel"),
        name="qkv_prep",
    )(proj, q_gain.reshape(1, HEAD_DIM), k_gain.reshape(1, HEAD_DIM), cos_t, sin_t)


def rope_tables(L):
    rows = L // GRID_W
    row_idx = jnp.repeat(jnp.arange(rows, dtype=F32), GRID_W)
    col_idx = jnp.tile(jnp.arange(GRID_W, dtype=F32), rows)
    inv_freq = ROPE_THETA ** (-jnp.arange(0, ROPE_AXIS_DIM, 2, dtype=F32) / ROPE_AXIS_DIM)
    ar = row_idx[:, None] * inv_freq[None, :]
    ac = col_idx[:, None] * inv_freq[None, :]
    cos_t = jnp.concatenate([jnp.cos(ar), jnp.cos(ar), jnp.cos(ac), jnp.cos(ac)], axis=-1)
    sin_t = jnp.concatenate([-jnp.sin(ar), jnp.sin(ar), -jnp.sin(ac), jnp.sin(ac)], axis=-1)
    return cos_t, sin_t


def _attn_kernel(q_ref, k_ref, v_ref, o_ref, m_sc, acc_sc, *, tk, n_kv):
    m_sc[...] = jnp.full(m_sc.shape, -jnp.inf, F32)
    acc_sc[...] = jnp.zeros(acc_sc.shape, F32)

    def body(j, carry):
        off = pl.multiple_of(j * tk, tk)
        k = k_ref[0, pl.ds(off, tk), :]
        v = v_ref[0, pl.ds(off, tk), :]
        for g in range(Q_PER_KV):
            q = q_ref[0, :, g * HEAD_DIM:(g + 1) * HEAD_DIM]
            s = lax.dot_general(q, k, (((1,), (1,)), ((), ())), preferred_element_type=F32)
            m_prev = m_sc[g]
            m_new = jnp.maximum(m_prev, jnp.max(s, axis=-1, keepdims=True))
            alpha = jnp.exp2(m_prev - m_new)
            p = jnp.exp2(s - jnp.tile(m_new, (1, tk // LANES)))
            pv = jnp.dot(p.astype(BF16), v, preferred_element_type=F32)
            acc_sc[g] = jnp.tile(alpha, (1, 2 * HEAD_DIM // LANES)) * acc_sc[g] + pv
            m_sc[g] = m_new
        return carry

    lax.fori_loop(0, n_kv, body, 0)
    for g in range(Q_PER_KV):
        acc = acc_sc[g]
        o_ref[0, :, g * HEAD_DIM:(g + 1) * HEAD_DIM] = acc[:, :HEAD_DIM] / acc[:, HEAD_DIM:]


def flash_attention(q, k, v1, tq=2048, tk=256):
    B, L, _ = q.shape
    tq, tk = _pick(L, tq), _pick(L, tk)
    gw = Q_PER_KV * HEAD_DIM
    kern = functools.partial(_attn_kernel, tk=tk, n_kv=L // tk)
    return pl.pallas_call(
        kern,
        grid=(B, N_KV_HEADS, L // tq),
        in_specs=[pl.BlockSpec((1, tq, gw), lambda b, h, i: (b, i, h)),
                  pl.BlockSpec((1, L, HEAD_DIM), lambda b, h, i: (b, 0, h)),
                  pl.BlockSpec((1, L, 2 * HEAD_DIM), lambda b, h, i: (b, 0, h))],
        out_specs=pl.BlockSpec((1, tq, gw), lambda b, h, i: (b, i, h)),
        out_shape=jax.ShapeDtypeStruct((B, L, ATTN_WIDTH), F32),
        scratch_shapes=[pltpu.VMEM((Q_PER_KV, tq, LANES), F32),
                        pltpu.VMEM((Q_PER_KV, tq, 2 * HEAD_DIM), F32)],
        compiler_params=_params("parallel", "parallel", "arbitrary"),
        name="flash_attention",
    )(q, k, v1)


def _hyena_pre_kernel(x0_ref, x1_ref, v_ref, x0p_ref, x1p_ref, vp_ref, x0n_ref, x1n_ref, vn_ref,
                      w0_ref, w1_ref, w2_ref, b0_ref, b1_ref, b2_ref, u_ref, x0o_ref, *, n_l):
    i = pl.program_id(0) % n_l
    first = i == 0
    last = i == n_l - 1
    tl = x0_ref.shape[0]
    row = lax.broadcasted_iota(jnp.int32, x0_ref.shape, 0)

    def conv(x_ref, p_ref, n_ref, w_ref, b_ref):
        x = x_ref[...]
        prev_row = jnp.where(first, 0.0, p_ref[SUBLANES - 1:SUBLANES, :])
        next_row = jnp.where(last, 0.0, n_ref[0:1, :])
        xm = jnp.where(row == 0, prev_row, pltpu.roll(x, 1, axis=0))
        xp = jnp.where(row == tl - 1, next_row, pltpu.roll(x, tl - 1, axis=0))
        return xm * w_ref[0:1, :] + x * w_ref[1:2, :] + xp * w_ref[2:3, :] + b_ref[...]

    x0o_ref[...] = conv(x0_ref, x0p_ref, x0n_ref, w0_ref, b0_ref)
    u_ref[...] = conv(v_ref, vp_ref, vn_ref, w2_ref, b2_ref) * conv(x1_ref, x1p_ref, x1n_ref, w1_ref, b1_ref)


def hyena_pre(proj, conv_w, conv_b, L, tl=512, tc=512):
    T = proj.shape[0]
    C = conv_w.shape[1] // 3
    tl, tc = _pick(L, tl), _pick(C, tc)
    n_l = L // tl
    nc = C // tc
    base = (ATTN_WIDTH + 2 * KV_WIDTH) // tc
    r8 = tl // SUBLANES
    n8 = T // SUBLANES

    def main(k):
        return pl.BlockSpec((tl, tc), lambda i, j: (i, base + k * nc + j))

    def prev(k):
        return pl.BlockSpec((SUBLANES, tc), lambda i, j: (jnp.maximum(i * r8 - 1, 0), base + k * nc + j))

    def nxt(k):
        return pl.BlockSpec((SUBLANES, tc), lambda i, j: (jnp.minimum((i + 1) * r8, n8 - 1), base + k * nc + j))

    def wsp(k):
        return pl.BlockSpec((SHORT_CONV, tc), lambda i, j: (0, k * nc + j))

    def bsp(k):
        return pl.BlockSpec((1, tc), lambda i, j: (0, k * nc + j))

    cb = conv_b.reshape(1, 3 * C)
    kern = functools.partial(_hyena_pre_kernel, n_l=n_l)
    return pl.pallas_call(
        kern,
        grid=(T // tl, nc),
        in_specs=[main(0), main(1), main(2), prev(0), prev(1), prev(2), nxt(0), nxt(1), nxt(2),
                  wsp(0), wsp(1), wsp(2), bsp(0), bsp(1), bsp(2)],
        out_specs=[pl.BlockSpec((tl, tc), lambda i, j: (i, j)),
                   pl.BlockSpec((tl, tc), lambda i, j: (i, j))],
        out_shape=[jax.ShapeDtypeStruct((T, C), F32), jax.ShapeDtypeStruct((T, C), F32)],
        compiler_params=_params("parallel", "parallel"),
        name="hyena_pre",
    )(proj, proj, proj, proj, proj, proj, proj, proj, proj,
      conv_w, conv_w, conv_w, cb, cb, cb)


def _filter_kernel(z_ref, t_ref, w1_ref, b1_ref, w2_ref, b2_ref, w3_ref, b3_ref, fr_ref,
                   wo_ref, dec_ref, wob_ref, decb_ref, o_ref, *, n_half):
    i = pl.program_id(0)
    hi = lax.Precision.HIGHEST
    fr = fr_ref[...]
    h = jnp.sin(fr * (jnp.dot(z_ref[...], w1_ref[...], precision=hi, preferred_element_type=F32) + b1_ref[...]))
    h = jnp.sin(fr * (jnp.dot(h, w2_ref[...], precision=hi, preferred_element_type=F32) + b2_ref[...]))
    h = jnp.sin(fr * (jnp.dot(h, w3_ref[...], precision=hi, preferred_element_type=F32) + b3_ref[...]))
    taps = jnp.dot(h, wo_ref[...], precision=hi, preferred_element_type=F32)
    o_ref[...] = taps * jnp.exp(-t_ref[...] * jnp.abs(dec_ref[...]))
    first_row = lax.broadcasted_iota(jnp.int32, (SUBLANES, o_ref.shape[1]), 0) == 0

    @pl.when(i == 0)
    def _():
        hb = jnp.dot(h[:SUBLANES], wob_ref[...], precision=hi, preferred_element_type=F32)
        hb = hb * jnp.exp(-t_ref[0:SUBLANES, :] * jnp.abs(decb_ref[...]))
        o_ref[0:SUBLANES, :] = o_ref[0:SUBLANES, :] + jnp.where(first_row, hb, 0.0)

    @pl.when(i == n_half)
    def _():
        o_ref[0:SUBLANES, :] = jnp.where(first_row, 0.0, o_ref[0:SUBLANES, :])


def filter_features(L):
    t = jnp.linspace(0.0, 1.0, L, dtype=F32)[:, None]
    w = 2.0 * math.pi * jnp.arange(L, dtype=F32) / L
    bands = jnp.linspace(1e-4, FILTER_BANDS - 1, FILTER_BANDS, dtype=F32)
    ang = w[:, None] * bands[None, :]
    z = jnp.concatenate([t, jnp.cos(ang), -jnp.sin(ang)], axis=-1)
    return z, t


def long_conv_kernel_rows(L, w1, b1, w2, b2, w3, b3, freq, w_out, decay, tl=512, tn=1024):
    z, t = filter_features(L)
    mirror = lambda a: jnp.concatenate([a, a[:1], a[:0:-1]], axis=0)
    z, t = mirror(z), mirror(t)
    emb, hid = w1.shape
    emb_pad = -(-emb // LANES) * LANES
    z = jnp.pad(z, ((0, 0), (0, emb_pad - emb)))
    w1 = jnp.pad(w1, ((0, emb_pad - emb), (0, 0)))
    C = w_out.shape[1] // 2
    tl, tn = _pick(L, tl), _pick(C, tn)
    n_half, n_c = L // tl, C // tn
    row = lambda a: a.reshape(1, -1)
    const = lambda shape: pl.BlockSpec(shape, lambda i, j: (0, 0))
    side = lambda i, j: (0, jnp.where(i < n_half, j, n_c + j))
    back = lambda i, j: (0, n_c + j)
    return pl.pallas_call(
        functools.partial(_filter_kernel, n_half=n_half),
        grid=(2 * n_half, n_c),
        in_specs=[pl.BlockSpec((tl, emb_pad), lambda i, j: (i, 0)),
                  pl.BlockSpec((tl, 1), lambda i, j: (i, 0)),
                  const((emb_pad, hid)), const((1, hid)),
                  const((hid, hid)), const((1, hid)),
                  const((hid, hid)), const((1, hid)),
                  const((1, hid)),
                  pl.BlockSpec((hid, tn), side), pl.BlockSpec((1, tn), side),
                  pl.BlockSpec((hid, tn), back), pl.BlockSpec((1, tn), back)],
        out_specs=pl.BlockSpec((tl, tn), lambda i, j: (i, j)),
        out_shape=jax.ShapeDtypeStruct((2 * L, C), F32),
        compiler_params=_params("parallel", "parallel"),
        name="implicit_filter",
    )(z, t, w1, row(b1), w2, row(b2), w3, row(b3), row(freq), w_out, row(decay), w_out, row(decay))


def _dft_consts(L):
    N = 2 * L
    N2 = FFT_N2
    N1 = N // N2
    H = N1 // 2
    k1 = np.arange(N1)
    f1 = np.exp(-2j * np.pi * np.outer(k1, k1) / N1)
    fh = f1[:, :H]
    m1_pair = np.block([[fh.real, -fh.imag], [fh.imag, fh.real]])
    m1_single = np.concatenate([fh.real, fh.imag], axis=0)
    m1_full = np.concatenate([f1.real, f1.imag], axis=0)
    n2 = np.arange(N2)
    f2 = np.exp(-2j * np.pi * np.outer(n2, n2) / N2)
    tw = np.exp(-2j * np.pi * np.outer(k1, n2) / N)
    c = lambda a: jnp.asarray(a, F32)
    return dict(N=N, N1=N1, N2=N2, H=H,
                m1_pair=c(m1_pair), m1_single=c(m1_single), m1_full=c(m1_full),
                m3_pair=c(m1_pair.T / N), m3_single=c(m1_single.T / N),
                f2r=c(f2.real), f2i=c(f2.imag),
                twr=c(tw.real).reshape(N1, 1, N2), twi=c(tw.imag).reshape(N1, 1, N2))


def _dft_stage1_kernel(m_ref, x_ref, o_ref):
    rows = m_ref.shape[1]
    tc = x_ref.shape[-1]
    m = m_ref[...].astype(BF16)
    x = pltpu.einshape("mnc->nmc", x_ref[...].reshape(rows, SUBLANES, tc))
    r = jnp.stack([jnp.dot(m, x[s].astype(BF16), preferred_element_type=F32) for s in range(SUBLANES)], axis=0)
    o_ref[...] = pltpu.einshape("nmc->mnc", r).reshape(o_ref.shape)


def dft_stage1(x4, m1, tc=256):
    S, R, N2, C = x4.shape
    N1 = m1.shape[0] // 2
    tc = _pick(C, tc)
    return pl.pallas_call(
        _dft_stage1_kernel,
        grid=(N2 // SUBLANES, C // tc),
        in_specs=[pl.BlockSpec(m1.shape, lambda j, c: (0, 0)),
                  pl.BlockSpec((S, R, SUBLANES, tc), lambda j, c: (0, 0, j, c))],
        out_specs=pl.BlockSpec((2, N1, SUBLANES, tc), lambda j, c: (0, 0, j, c)),
        out_shape=jax.ShapeDtypeStruct((2, N1, N2, C), F32),
        compiler_params=_params("parallel", "parallel"),
        name="dft_stage1",
    )(m1, x4)


def _stage2_matrix(f2r_ref, f2i_ref, twr_ref, twi_ref):
    f2r, f2i = f2r_ref[...], f2i_ref[...]
    twr, twi = twr_ref[...], twi_ref[...]
    er = f2r * twr - f2i * twi
    ei = f2r * twi + f2i * twr
    return jnp.concatenate([jnp.concatenate([er, -ei], axis=1),
                            jnp.concatenate([ei, er], axis=1)], axis=0)


def _dft_stage2_kernel(f2r_ref, f2i_ref, twr_ref, twi_ref, a_ref, o_ref):
    n2, c = a_ref.shape[1], a_ref.shape[2]
    m = _stage2_matrix(f2r_ref, f2i_ref, twr_ref, twi_ref).astype(BF16)
    a = a_ref[...].reshape(2 * n2, c).astype(BF16)
    o_ref[...] = jnp.dot(m, a, preferred_element_type=F32).reshape(o_ref.shape)


def _spec_blocks(N1, N2, C):
    mat = pl.BlockSpec((N2, N2), lambda i: (0, 0))
    tw = pl.BlockSpec((None, 1, N2), lambda i: (i, 0, 0))
    dat = pl.BlockSpec((2, None, N2, C), lambda i: (0, i, 0, 0))
    return mat, tw, dat


def dft_stage2(a, cst):
    _, N1, N2, C = a.shape
    mat, tw, dat = _spec_blocks(N1, N2, C)
    return pl.pallas_call(
        _dft_stage2_kernel,
        grid=(N1,),
        in_specs=[mat, mat, tw, tw, dat],
        out_specs=dat,
        out_shape=jax.ShapeDtypeStruct(a.shape, F32),
        compiler_params=_params("parallel"),
        name="dft_stage2",
    )(cst["f2r"], cst["f2i"], cst["twr"], cst["twi"], a)


def _spectral_kernel(f2r_ref, f2i_ref, twr_ref, twi_ref, a_ref, k_ref, o_ref):
    n2, c = a_ref.shape[1], a_ref.shape[2]
    m = _stage2_matrix(f2r_ref, f2i_ref, twr_ref, twi_ref)
    a = a_ref[...].reshape(2 * n2, c).astype(BF16)
    x = jnp.dot(m.astype(BF16), a, preferred_element_type=F32)
    xr, xi = x[:n2], x[n2:]
    kr, ki = k_ref[0], k_ref[1]
    y = jnp.concatenate([xr * kr - xi * ki, xr * ki + xi * kr], axis=0).astype(BF16)
    o_ref[...] = jnp.dot(m.T.astype(BF16), y, preferred_element_type=F32).reshape(o_ref.shape)


def spectral_multiply(a, kspec, cst):
    _, N1, N2, C = a.shape
    mat, tw, dat = _spec_blocks(N1, N2, C)
    return pl.pallas_call(
        _spectral_kernel,
        grid=(N1,),
        in_specs=[mat, mat, tw, tw, dat, dat],
        out_specs=dat,
        out_shape=jax.ShapeDtypeStruct(a.shape, F32),
        compiler_params=_params("parallel"),
        name="spectral_multiply",
    )(cst["f2r"], cst["f2i"], cst["twr"], cst["twi"], a, kspec)


def _dft_final_kernel(m_ref, b_ref, u_ref, x0_ref, bias_ref, o_ref):
    rows = m_ref.shape[1]
    c = b_ref.shape[-1]
    m = m_ref[...].astype(BF16)
    b = pltpu.einshape("mnc->nmc", b_ref[...].reshape(rows, SUBLANES, c))
    y = jnp.stack([jnp.dot(m, b[s].astype(BF16), preferred_element_type=F32) for s in range(SUBLANES)], axis=0)
    y = pltpu.einshape("nmc->mnc", y).reshape(o_ref.shape)
    o_ref[...] = (y + u_ref[...] * bias_ref[...]) * x0_ref[...]


def dft_final(b, m3, u4, x04, bias, tc=256):
    S, R, N2, C = u4.shape
    N1 = b.shape[1]
    tc = _pick(C, tc)
    seq = pl.BlockSpec((S, R, SUBLANES, tc), lambda j, c: (0, 0, j, c))
    return pl.pallas_call(
        _dft_final_kernel,
        grid=(N2 // SUBLANES, C // tc),
        in_specs=[pl.BlockSpec(m3.shape, lambda j, c: (0, 0)),
                  pl.BlockSpec((2, N1, SUBLANES, tc), lambda j, c: (0, 0, j, c)),
                  seq, seq,
                  pl.BlockSpec((1, tc), lambda j, c: (0, c))],
        out_specs=seq,
        out_shape=jax.ShapeDtypeStruct(u4.shape, F32),
        compiler_params=_params("parallel", "parallel"),
        name="dft_final",
    )(m3, b, u4, x04, bias.reshape(1, C))


def long_conv_gate(u, x0, kern, bias, n_pair):
    B, L, C = u.shape
    cst = _dft_consts(L)
    N1, N2, H = cst["N1"], cst["N2"], cst["H"]
    kspec = dft_stage2(dft_stage1(kern.reshape(1, N1, N2, C), cst["m1_full"]), cst)
    outs = []
    for lo, hi in ((0, n_pair), (n_pair, B)):
        if hi == lo:
            continue
        pair = (hi - lo) == 2
        u4 = u[lo:hi].reshape(hi - lo, H, N2, C)
        x04 = x0[lo:hi].reshape(hi - lo, H, N2, C)
        a = dft_stage1(u4, cst["m1_pair"] if pair else cst["m1_single"])
        bsp = spectral_multiply(a, kspec, cst)
        y = dft_final(bsp, cst["m3_pair"] if pair else cst["m3_single"], u4, x04, bias)
        outs.append(y.reshape(hi - lo, L, C))
    return jnp.concatenate(outs, axis=0) if len(outs) > 1 else outs[0]


def _pack_bf16_pairs(h):
    half = h.shape[1] // 2
    bits = lax.bitcast_convert_type(h.astype(BF16).astype(F32), jnp.uint32)
    return bits[:, :half] | (bits[:, half:] >> 16)


def _unpack_bf16_pairs(w):
    hi = lax.bitcast_convert_type(w & jnp.uint32(0xFFFF0000), F32).astype(BF16)
    lo = lax.bitcast_convert_type(w << 16, F32).astype(BF16)
    return hi, lo


def _post_mixer_kernel(xa_ref, xb_ref, a_ref, hy_ref, ga_ref, gh_ref, wo_ref, gf_ref, wr_ref, br_ref,
                       x1_ref, h_ref, idx_ref, gate_ref, *, n_a):
    merged = jnp.concatenate([_rms(a_ref[...], ga_ref[...]), _rms(hy_ref[...], gh_ref[...])], axis=-1)
    x = jnp.where(pl.program_id(0) < n_a, xa_ref[...], xb_ref[...])
    x1 = x + jnp.dot(merged.astype(BF16), wo_ref[...], preferred_element_type=F32)
    x1_ref[...] = x1
    h = _rms(x1, gf_ref[...])
    h_ref[...] = _pack_bf16_pairs(h)
    logits = lax.dot_general(wr_ref[...], h, (((1,), (1,)), ((), ())),
                             precision=lax.Precision.HIGHEST, preferred_element_type=F32) + br_ref[...]
    e_iota = lax.broadcasted_iota(jnp.int32, logits.shape, 0).astype(F32)
    vals, idxs = [], []
    cur = logits
    for _ in range(TOP_K):
        m = jnp.max(cur, axis=0, keepdims=True)
        sel = jnp.min(jnp.where(cur == m, e_iota, float(N_EXPERTS)), axis=0, keepdims=True)
        vals.append(m)
        idxs.append(sel)
        cur = jnp.where(e_iota == sel, -jnp.inf, cur)
    ex = [jnp.exp(v - vals[0]) for v in vals]
    den = ex[0] + ex[1] + ex[2] + ex[3]
    idx_ref[...] = jnp.concatenate(idxs, axis=0).astype(jnp.int32)
    gate_ref[...] = jnp.concatenate([e / den for e in ex], axis=0)


def post_mixer(xa, xb, attn, hy, g_attn, g_hy, w_out, g_ffn, w_router_t, b_router, tm=512):
    Ta, D = xa.shape
    T = Ta + xb.shape[0]
    Wa, Wh = attn.shape[1], hy.shape[1]
    tm = _pick(math.gcd(Ta, xb.shape[0]), tm)
    n_a = Ta // tm
    spec_a, spec_b = _row_tiles_of_two(tm, D, n_a)
    const = lambda shape: pl.BlockSpec(shape, lambda i: (0, 0))
    rows = lambda w: pl.BlockSpec((tm, w), lambda i: (i, 0))
    cols = pl.BlockSpec((TOP_K, tm), lambda i: (0, i))
    return pl.pallas_call(
        functools.partial(_post_mixer_kernel, n_a=n_a),
        grid=(T // tm,),
        in_specs=[spec_a, spec_b, rows(Wa), rows(Wh), const((1, Wa)), const((1, Wh)), const((Wa + Wh, D)),
                  const((1, D)), const((N_EXPERTS, D)), const((N_EXPERTS, 1))],
        out_specs=[rows(D), rows(D // 2), cols, cols],
        out_shape=[jax.ShapeDtypeStruct((T, D), F32), jax.ShapeDtypeStruct((T, D // 2), jnp.uint32),
                   jax.ShapeDtypeStruct((TOP_K, T), jnp.int32), jax.ShapeDtypeStruct((TOP_K, T), F32)],
        compiler_params=_params("parallel"),
        name="post_mixer",
    )(xa, xb, attn, hy, g_attn.reshape(1, Wa), g_hy.reshape(1, Wh), w_out, g_ffn.reshape(1, D),
      w_router_t, b_router.reshape(N_EXPERTS, 1))


def route(top_idx, top_gate, tile):
    K, T = top_idx.shape
    n_assign = K * T
    e_flat = top_idx.reshape(-1)
    pos = jnp.arange(n_assign, dtype=jnp.int32)
    e_sorted, order, gate_sorted = lax.sort((e_flat, pos, top_gate.reshape(-1)), num_keys=1, is_stable=True)
    tok_sorted = order % T
    counts = jnp.sum((e_flat[None, :] == jnp.arange(N_EXPERTS, dtype=jnp.int32)[:, None]).astype(jnp.int32), axis=1)
    start = jnp.cumsum(counts) - counts
    padded = (counts + tile - 1) // tile * tile
    pend = jnp.cumsum(padded)
    pstart = pend - padded
    n_blocks = n_assign // tile + N_EXPERTS
    n_slots = n_blocks * tile
    block_start = jnp.arange(n_blocks, dtype=jnp.int32) * tile
    n_used = (pend[-1] // tile).astype(jnp.int32)
    block_expert = jnp.minimum(jnp.sum(block_start[:, None] >= pend[None, :], axis=1), N_EXPERTS - 1)
    last_expert = block_expert[jnp.maximum(n_used - 1, 0)]
    block_expert = jnp.where(jnp.arange(n_blocks) < n_used, block_expert, last_expert).astype(jnp.int32)
    slot = jnp.arange(n_slots, dtype=jnp.int32)
    slot_e = jnp.repeat(block_expert, tile)
    within = slot - pstart[slot_e]
    valid = jnp.logical_and(within < counts[slot_e], slot < pend[-1])
    src = jnp.clip(start[slot_e] + within, 0, n_assign - 1)
    slot_tok = jnp.where(valid, tok_sorted[src], 0)
    slot_gate = jnp.where(valid, gate_sorted[src], 0.0)
    slot_sorted = pstart[e_sorted] + (pos - start[e_sorted])
    _, slot_of = lax.sort((order, slot_sorted), num_keys=1)
    return slot_tok, slot_gate, slot_of.reshape(K, T), block_expert, n_used.reshape(1)


def _row_copy(src_ref, src_row, dst_ref, dst_row, sem):
    return pltpu.make_async_copy(src_ref.at[pl.ds(src_row, 1), :], dst_ref.at[pl.ds(dst_row, 1), :], sem)


DMA_ISSUE_UNROLL = 8


def _gather_kernel(nrows_ref, idx_ref, src_ref, dst_ref, sem, *, rows):
    live = pl.program_id(0) * rows < nrows_ref[0]

    @pl.when(live)
    def _():
        def issue(r, c):
            _row_copy(src_ref, idx_ref[0, 0, r], dst_ref, r, sem).start()
            return c

        lax.fori_loop(0, rows, issue, 0, unroll=DMA_ISSUE_UNROLL)
        pltpu.make_async_copy(src_ref.at[pl.ds(0, rows), :], dst_ref, sem).wait()

    @pl.when(jnp.logical_not(live))
    def _():
        dst_ref[...] = jnp.zeros(dst_ref.shape, dst_ref.dtype)


def gather_rows(src, idx, n_live_rows, rows=512):
    T, W = src.shape
    n = idx.shape[0]
    rows = _pick(n, rows)
    return pl.pallas_call(
        functools.partial(_gather_kernel, rows=rows),
        grid=(n // rows,),
        in_specs=[pl.BlockSpec(memory_space=pltpu.SMEM),
                  pl.BlockSpec((1, 1, rows), lambda i: (i, 0, 0), memory_space=pltpu.SMEM),
                  pl.BlockSpec(memory_space=pl.ANY)],
        out_specs=pl.BlockSpec((rows, W), lambda i: (i, 0)),
        out_shape=jax.ShapeDtypeStruct((n, W), src.dtype),
        scratch_shapes=[pltpu.SemaphoreType.DMA(())],
        compiler_params=_params("arbitrary"),
        name="gather_rows",
    )(n_live_rows, idx.reshape(n // rows, 1, rows), src)


def _expert_kernel(be_ref, nu_ref, x_ref, wg_ref, wl_ref, bg_ref, bl_ref, wd_ref, bd_ref, g_ref,
                   o_ref, xs_sc, *, n_f):
    b = pl.program_id(0)
    f = pl.program_id(1)
    half = x_ref.shape[1]
    used = b < nu_ref[0]

    @pl.when(jnp.logical_and(used, f == 0))
    def _():
        hi, lo = _unpack_bf16_pairs(x_ref[...])
        xs_sc[:, :half] = hi
        xs_sc[:, half:] = lo

    @pl.when(used)
    def _():
        x = xs_sc[...]
        glu = jnp.dot(x, wg_ref[0], preferred_element_type=F32) + bg_ref[0]
        lin = jnp.dot(x, wl_ref[0], preferred_element_type=F32) + bl_ref[0]
        glu = jnp.minimum(glu, SWIGLU_LIMIT)
        lin = jnp.clip(lin, -SWIGLU_LIMIT, SWIGLU_LIMIT)
        y = glu * jax.nn.sigmoid(SWIGLU_ALPHA * glu) * (lin + 1.0)
        part = jnp.dot(y.astype(BF16), wd_ref[0], preferred_element_type=F32)

        @pl.when(f == 0)
        def _():
            o_ref[...] = part

        @pl.when(f > 0)
        def _():
            o_ref[...] += part

        @pl.when(f == n_f - 1)
        def _():
            g = g_ref[...]
            for c in range(o_ref.shape[1] // LANES):
                sl = slice(c * LANES, (c + 1) * LANES)
                o_ref[:, sl] = (o_ref[:, sl] + bd_ref[0, :, sl]) * g

    @pl.when(jnp.logical_and(jnp.logical_not(used), f == n_f - 1))
    def _():
        o_ref[...] = jnp.zeros(o_ref.shape, o_ref.dtype)


def expert_mlp(x_slots, block_expert, n_used, w_up, b_up, w_down, b_down, slot_gate,
               tile=EXPERT_TILE, tf=FF_TILE):
    n_slots, half = x_slots.shape
    D = 2 * half
    E, _, two_ff = w_up.shape
    d_ff = two_ff // 2
    tf = _pick(d_ff, tf)
    n_f = d_ff // tf
    n_blocks = n_slots // tile
    gate_rep = jnp.broadcast_to(slot_gate[:, None], (n_slots, LANES))

    def blk(b, nu):
        return jnp.minimum(b, jnp.maximum(nu[0] - 1, 0))

    def ff(b, f, nu):
        return jnp.where(b < nu[0], f, n_f - 1)

    grid_spec = pltpu.PrefetchScalarGridSpec(
        num_scalar_prefetch=2,
        grid=(n_blocks, n_f),
        in_specs=[pl.BlockSpec((tile, half), lambda b, f, be, nu: (blk(b, nu), 0)),
                  pl.BlockSpec((1, D, tf), lambda b, f, be, nu: (be[b], 0, ff(b, f, nu))),
                  pl.BlockSpec((1, D, tf), lambda b, f, be, nu: (be[b], 0, n_f + ff(b, f, nu))),
                  pl.BlockSpec((1, 1, tf), lambda b, f, be, nu: (be[b], 0, ff(b, f, nu))),
                  pl.BlockSpec((1, 1, tf), lambda b, f, be, nu: (be[b], 0, n_f + ff(b, f, nu))),
                  pl.BlockSpec((1, tf, D), lambda b, f, be, nu: (be[b], ff(b, f, nu), 0)),
                  pl.BlockSpec((1, 1, D), lambda b, f, be, nu: (be[b], 0, 0)),
                  pl.BlockSpec((tile, LANES), lambda b, f, be, nu: (blk(b, nu), 0))],
        out_specs=pl.BlockSpec((tile, D), lambda b, f, be, nu: (b, 0)),
        scratch_shapes=[pltpu.VMEM((tile, D), BF16)])
    return pl.pallas_call(
        functools.partial(_expert_kernel, n_f=n_f),
        grid_spec=grid_spec,
        out_shape=jax.ShapeDtypeStruct((n_slots, D), F32),
        compiler_params=_params("arbitrary", "arbitrary"),
        name="expert_mlp",
    )(block_expert, n_used, x_slots, w_up, w_up, b_up.reshape(E, 1, two_ff), b_up.reshape(E, 1, two_ff),
      w_down, b_down.reshape(E, 1, D), gate_rep)


def _final_kernel(slot_ref, y_hbm, x1_ref, pa_ref, pb_ref, gp_ref, wg_ref, wp_ref, gf_ref, oa_ref, ob_ref,
                  rows_sc, sem, *, tm, n_a):
    def issue(r, c):
        for k in range(TOP_K):
            _row_copy(y_hbm, slot_ref[0, k, r], rows_sc.at[k], r, sem).start()
        return c

    lax.fori_loop(0, tm, issue, 0, unroll=DMA_ISSUE_UNROLL // TOP_K)
    for k in range(TOP_K):
        pltpu.make_async_copy(y_hbm.at[pl.ds(0, tm), :], rows_sc.at[k], sem).wait()

    x2 = x1_ref[...] + ((rows_sc[0] + rows_sc[1]) + (rows_sc[2] + rows_sc[3]))
    gate = jax.nn.sigmoid(jnp.dot(_rms(x2, gp_ref[...]).astype(BF16), wg_ref[...],
                                  preferred_element_type=F32))
    from_a = pl.program_id(0) < n_a
    p = jnp.where(from_a, pa_ref[...], pb_ref[...])
    x3 = x2 + gate * jnp.dot(p.astype(BF16), wp_ref[...], preferred_element_type=F32)
    y = _rms(x3, gf_ref[...])

    @pl.when(from_a)
    def _():
        oa_ref[...] = y

    @pl.when(jnp.logical_not(from_a))
    def _():
        ob_ref[...] = y


def combine_ple_final(y_slots, slot_of, x1, pa, pb, g_ple, w_gate, w_proj, g_final, tm=256):
    T, D = x1.shape
    Ta, P = pa.shape
    Tb = pb.shape[0]
    tm = _pick(math.gcd(Ta, Tb), tm)
    n_a = Ta // tm
    pa_spec, pb_spec = _row_tiles_of_two(tm, P, n_a)
    oa_spec, ob_spec = _row_tiles_of_two(tm, D, n_a)
    slots = slot_of.reshape(TOP_K, T // tm, tm).transpose(1, 0, 2)
    const = lambda shape: pl.BlockSpec(shape, lambda i: (0, 0))
    return pl.pallas_call(
        functools.partial(_final_kernel, tm=tm, n_a=n_a),
        grid=(T // tm,),
        in_specs=[pl.BlockSpec((1, TOP_K, tm), lambda i: (i, 0, 0), memory_space=pltpu.SMEM),
                  pl.BlockSpec(memory_space=pl.ANY),
                  pl.BlockSpec((tm, D), lambda i: (i, 0)),
                  pa_spec, pb_spec,
                  const((1, D)), const((D, D)), const((P, D)), const((1, D))],
        out_specs=[oa_spec, ob_spec],
        out_shape=[jax.ShapeDtypeStruct((Ta, D), F32), jax.ShapeDtypeStruct((Tb, D), F32)],
        scratch_shapes=[pltpu.VMEM((TOP_K, tm, D), F32),
                        pltpu.SemaphoreType.DMA(())],
        compiler_params=_params("arbitrary"),
        name="combine_ple_final",
    )(slots, y_slots, x1, pa, pb, g_ple.reshape(1, D), w_gate, w_proj, g_final.reshape(1, D))


def kernel(x_prompt, x_sample, p_prompt, p_sample, g_mix, w_in, q_gain, k_gain, conv_w, conv_b, filt_w1, filt_b1, filt_w2, filt_b2, filt_w3, filt_b3, filt_freq, filt_w_out, filt_decay, hyena_bias, g_attn_out, g_hyena_out, w_out, g_ffn, w_router, b_router, w_up, b_up, w_down, b_down, g_ple, w_ple_gate, w_ple_proj, g_final):
    depth = g_mix.shape[0]
    bp, L, D = x_prompt.shape
    bs = x_sample.shape[0]
    B = bp + bs
    T = B * L
    C = hyena_bias.shape[1]
    assert depth == 1
    xa, xb = x_prompt.reshape(bp * L, D), x_sample.reshape(bs * L, D)
    cos_t, sin_t = rope_tables(L)
    n_pair = bp if bp == 2 else 0

    for i in range(depth):
        proj = norm_matmul(xa, xb, g_mix[i], w_in[i].astype(BF16))
        q, k, v = qkv_prep(proj, q_gain[i], k_gain[i], cos_t, sin_t, L)
        attn = flash_attention(q.reshape(B, L, ATTN_WIDTH), k.reshape(B, L, KV_WIDTH),
                               v.reshape(B, L, 2 * KV_WIDTH)).reshape(T, ATTN_WIDTH)
        u, x0 = hyena_pre(proj, conv_w[i], conv_b[i], L)
        kern = long_conv_kernel_rows(L, filt_w1[i], filt_b1[i], filt_w2[i], filt_b2[i], filt_w3[i], filt_b3[i],
                                     filt_freq[i], filt_w_out[i], filt_decay[i])
        hy = long_conv_gate(u.reshape(B, L, C), x0.reshape(B, L, C), kern, hyena_bias[i], n_pair).reshape(T, C)
        x1, h, top_idx, top_gate = post_mixer(xa, xb, attn, hy, g_attn_out[i], g_hyena_out[i],
                                              w_out[i].astype(BF16), g_ffn[i], w_router[i].T, b_router[i])
        slot_tok, slot_gate, slot_of, block_expert, n_used = route(top_idx, top_gate, EXPERT_TILE)
        x_slots = gather_rows(h, slot_tok, n_used * EXPERT_TILE)
        y_slots = expert_mlp(x_slots, block_expert, n_used, w_up[i].astype(BF16), b_up[i],
                             w_down[i].astype(BF16), b_down[i], slot_gate)
        pa = p_prompt[i].reshape(bp * L, -1)
        pb = p_sample[i].reshape(bs * L, -1)
        ya, yb = combine_ple_final(y_slots, slot_of, x1, pa, pb, g_ple[i],
                                   w_ple_gate[i].astype(BF16), w_ple_proj[i].astype(BF16), g_final)
    return (ya.reshape(bp, L, D), yb.reshape(bs, L, D))
```

```python
import functools
import math

import numpy as np
import jax
import jax.numpy as jnp
from jax import lax
from jax.experimental import pallas as pl
from jax.experimental.pallas import tpu as pltpu

F32 = jnp.float32
BF16 = jnp.bfloat16

HEAD_DIM = 128
N_Q_HEADS = 8
N_KV_HEADS = 2
Q_PER_KV = N_Q_HEADS // N_KV_HEADS
ATTN_WIDTH = N_Q_HEADS * HEAD_DIM
KV_WIDTH = N_KV_HEADS * HEAD_DIM
GRID_W = 64
ROPE_AXIS_DIM = HEAD_DIM // 2
ROPE_THETA = 10000.0
SHORT_CONV = 3
FILTER_EMB = 33
FILTER_BANDS = (FILTER_EMB - 1) // 2
N_EXPERTS = 32
TOP_K = 4
SWIGLU_ALPHA = 1.702
SWIGLU_LIMIT = 7.0
EPS = 1e-6
LOG2E = 1.4426950408889634

LANES = 128
SUBLANES = 8
VMEM_LIMIT_BYTES = 56 * 1024 * 1024

FFT_N2 = 128
EXPERT_TILE = 512
FF_TILE = 1024


def _params(*sem):
    return pltpu.CompilerParams(dimension_semantics=sem, vmem_limit_bytes=VMEM_LIMIT_BYTES)


def _pick(n, want):
    t = min(n, want)
    while n % t:
        t //= 2
    return t


def _rms(x, g):
    return x * lax.rsqrt(jnp.mean(x * x, axis=-1, keepdims=True) + EPS) * g


def _row_tiles_of_two(tm, width, n_a):
    a = pl.BlockSpec((tm, width), lambda i, *_: (jnp.minimum(i, n_a - 1), 0))
    b = pl.BlockSpec((tm, width), lambda i, *_: (jnp.maximum(i - n_a, 0), 0))
    return a, b


def _norm_matmul_kernel(xa_ref, xb_ref, g_ref, w_ref, o_ref, h_sc, *, n_a):
    @pl.when(pl.program_id(1) == 0)
    def _():
        x = jnp.where(pl.program_id(0) < n_a, xa_ref[...], xb_ref[...])
        h_sc[...] = _rms(x, g_ref[...]).astype(BF16)

    o_ref[...] = jnp.dot(h_sc[...], w_ref[...], preferred_element_type=F32)


def norm_matmul(xa, xb, g, w, tm=1024, tn=512):
    Ta, D = xa.shape
    Tb = xb.shape[0]
    N = w.shape[1]
    tm, tn = _pick(math.gcd(Ta, Tb), tm), _pick(N, tn)
    n_a = Ta // tm
    spec_a, spec_b = _row_tiles_of_two(tm, D, n_a)
    return pl.pallas_call(
        functools.partial(_norm_matmul_kernel, n_a=n_a),
        grid=((Ta + Tb) // tm, N // tn),
        in_specs=[spec_a, spec_b,
                  pl.BlockSpec((1, D), lambda i, j: (0, 0)),
                  pl.BlockSpec((D, tn), lambda i, j: (0, j))],
        out_specs=pl.BlockSpec((tm, tn), lambda i, j: (i, j)),
        out_shape=jax.ShapeDtypeStruct((Ta + Tb, N), F32),
        scratch_shapes=[pltpu.VMEM((tm, D), BF16)],
        compiler_params=_params("parallel", "arbitrary"),
        name="norm_matmul",
    )(xa, xb, g.reshape(1, D), w)


def _qkv_prep_kernel(p_ref, qg_ref, kg_ref, c_ref, s_ref, q_ref, k_ref, v_ref):
    cos = c_ref[...]
    sin = s_ref[...]
    lane = lax.broadcasted_iota(jnp.int32, cos.shape, 1)
    low_half = (lane % ROPE_AXIS_DIM) < (ROPE_AXIS_DIM // 2)

    def norm_rope(x, g):
        y = _rms(x, g)
        fwd = pltpu.roll(y, ROPE_AXIS_DIM // 2, axis=1)
        bwd = pltpu.roll(y, HEAD_DIM - ROPE_AXIS_DIM // 2, axis=1)
        return y * cos + jnp.where(low_half, bwd, fwd) * sin

    q_scale = (HEAD_DIM ** -0.5) * LOG2E
    for h in range(N_Q_HEADS):
        sl = slice(h * HEAD_DIM, (h + 1) * HEAD_DIM)
        q_ref[:, sl] = (norm_rope(p_ref[:, sl], qg_ref[...]) * q_scale).astype(BF16)
    for h in range(N_KV_HEADS):
        src = slice(ATTN_WIDTH + h * HEAD_DIM, ATTN_WIDTH + (h + 1) * HEAD_DIM)
        dst = slice(h * HEAD_DIM, (h + 1) * HEAD_DIM)
        k_ref[:, dst] = norm_rope(p_ref[:, src], kg_ref[...]).astype(BF16)
    ones = jnp.ones((p_ref.shape[0], HEAD_DIM), BF16)
    for h in range(N_KV_HEADS):
        src = slice(ATTN_WIDTH + KV_WIDTH + h * HEAD_DIM, ATTN_WIDTH + KV_WIDTH + (h + 1) * HEAD_DIM)
        v_ref[:, 2 * h * HEAD_DIM:(2 * h + 1) * HEAD_DIM] = p_ref[:, src].astype(BF16)
        v_ref[:, (2 * h + 1) * HEAD_DIM:(2 * h + 2) * HEAD_DIM] = ones


def qkv_prep(proj, q_gain, k_gain, cos_t, sin_t, L, tl=512):
    T = proj.shape[0]
    tl = _pick(L, tl)
    nl = L // tl
    w_qkv = ATTN_WIDTH + 2 * KV_WIDTH
    return pl.pallas_call(
        _qkv_prep_kernel,
        grid=(T // tl,),
        in_specs=[pl.BlockSpec((tl, w_qkv), lambda i: (i, 0)),
                  pl.BlockSpec((1, HEAD_DIM), lambda i: (0, 0)),
                  pl.BlockSpec((1, HEAD_DIM), lambda i: (0, 0)),
                  pl.BlockSpec((tl, HEAD_DIM), lambda i: (i % nl, 0)),
                  pl.BlockSpec((tl, HEAD_DIM), lambda i: (i % nl, 0))],
        out_specs=[pl.BlockSpec((tl, ATTN_WIDTH), lambda i: (i, 0)),
                   pl.BlockSpec((tl, KV_WIDTH), lambda i: (i, 0)),
                   pl.BlockSpec((tl, 2 * KV_WIDTH), lambda i: (i, 0))],
        out_shape=[jax.ShapeDtypeStruct((T, ATTN_WIDTH), BF16),
                   jax.ShapeDtypeStruct((T, KV_WIDTH), BF16),
                   jax.ShapeDtypeStruct((T, 2 * KV_WIDTH), BF16)],
        compiler_params=_params("parallel"),
        name="qkv_prep",
    )(proj, q_gain.reshape(1, HEAD_DIM), k_gain.reshape(1, HEAD_DIM), cos_t, sin_t)


def rope_tables(L):
    rows = L // GRID_W
    row_idx = jnp.repeat(jnp.arange(rows, dtype=F32), GRID_W)
    col_idx = jnp.tile(jnp.arange(GRID_W, dtype=F32), rows)
    inv_freq = ROPE_THETA ** (-jnp.arange(0, ROPE_AXIS_DIM, 2, dtype=F32) / ROPE_AXIS_DIM)
    ar = row_idx[:, None] * inv_freq[None, :]
    ac = col_idx[:, None] * inv_freq[None, :]
    cos_t = jnp.concatenate([jnp.cos(ar), jnp.cos(ar), jnp.cos(ac), jnp.cos(ac)], axis=-1)
    sin_t = jnp.concatenate([-jnp.sin(ar), jnp.sin(ar), -jnp.sin(ac), jnp.sin(ac)], axis=-1)
    return cos_t, sin_t


def _attn_kernel(q_ref, k_ref, v_ref, o_ref, m_sc, acc_sc, *, tk, n_kv):
    m_sc[...] = jnp.full(m_sc.shape, -jnp.inf, F32)
    acc_sc[...] = jnp.zeros(acc_sc.shape, F32)

    def body(j, carry):
        off = pl.multiple_of(j * tk, tk)
        k = k_ref[0, pl.ds(off, tk), :]
        v = v_ref[0, pl.ds(off, tk), :]
        for g in range(Q_PER_KV):
            q = q_ref[0, :, g * HEAD_DIM:(g + 1) * HEAD_DIM]
            s = lax.dot_general(q, k, (((1,), (1,)), ((), ())), preferred_element_type=F32)
            m_prev = m_sc[g]
            m_new = jnp.maximum(m_prev, jnp.max(s, axis=-1, keepdims=True))
            alpha = jnp.exp2(m_prev - m_new)
            p = jnp.exp2(s - jnp.tile(m_new, (1, tk // LANES)))
            pv = jnp.dot(p.astype(BF16), v, preferred_element_type=F32)
            acc_sc[g] = jnp.tile(alpha, (1, 2 * HEAD_DIM // LANES)) * acc_sc[g] + pv
            m_sc[g] = m_new
        return carry

    lax.fori_loop(0, n_kv, body, 0)
    for g in range(Q_PER_KV):
        acc = acc_sc[g]
        o_ref[0, :, g * HEAD_DIM:(g + 1) * HEAD_DIM] = acc[:, :HEAD_DIM] / acc[:, HEAD_DIM:]


def flash_attention(q, k, v1, tq=2048, tk=256):
    B, L, _ = q.shape
    tq, tk = _pick(L, tq), _pick(L, tk)
    gw = Q_PER_KV * HEAD_DIM
    kern = functools.partial(_attn_kernel, tk=tk, n_kv=L // tk)
    return pl.pallas_call(
        kern,
        grid=(B, N_KV_HEADS, L // tq),
        in_specs=[pl.BlockSpec((1, tq, gw), lambda b, h, i: (b, i, h)),
                  pl.BlockSpec((1, L, HEAD_DIM), lambda b, h, i: (b, 0, h)),
                  pl.BlockSpec((1, L, 2 * HEAD_DIM), lambda b, h, i: (b, 0, h))],
        out_specs=pl.BlockSpec((1, tq, gw), lambda b, h, i: (b, i, h)),
        out_shape=jax.ShapeDtypeStruct((B, L, ATTN_WIDTH), F32),
        scratch_shapes=[pltpu.VMEM((Q_PER_KV, tq, LANES), F32),
                        pltpu.VMEM((Q_PER_KV, tq, 2 * HEAD_DIM), F32)],
        compiler_params=_params("parallel", "parallel", "arbitrary"),
        name="flash_attention",
    )(q, k, v1)


def _hyena_pre_kernel(x0_ref, x1_ref, v_ref, x0p_ref, x1p_ref, vp_ref, x0n_ref, x1n_ref, vn_ref,
                      w0_ref, w1_ref, w2_ref, b0_ref, b1_ref, b2_ref, u_ref, x0o_ref, *, n_l):
    i = pl.program_id(0) % n_l
    first = i == 0
    last = i == n_l - 1
    tl = x0_ref.shape[0]
    row = lax.broadcasted_iota(jnp.int32, x0_ref.shape, 0)

    def conv(x_ref, p_ref, n_ref, w_ref, b_ref):
        x = x_ref[...]
        prev_row = jnp.where(first, 0.0, p_ref[SUBLANES - 1:SUBLANES, :])
        next_row = jnp.where(last, 0.0, n_ref[0:1, :])
        xm = jnp.where(row == 0, prev_row, pltpu.roll(x, 1, axis=0))
        xp = jnp.where(row == tl - 1, next_row, pltpu.roll(x, tl - 1, axis=0))
        return xm * w_ref[0:1, :] + x * w_ref[1:2, :] + xp * w_ref[2:3, :] + b_ref[...]

    x0o_ref[...] = conv(x0_ref, x0p_ref, x0n_ref, w0_ref, b0_ref)
    u_ref[...] = conv(v_ref, vp_ref, vn_ref, w2_ref, b2_ref) * conv(x1_ref, x1p_ref, x1n_ref, w1_ref, b1_ref)


def hyena_pre(proj, conv_w, conv_b, L, tl=512, tc=512):
    T = proj.shape[0]
    C = conv_w.shape[1] // 3
    tl, tc = _pick(L, tl), _pick(C, tc)
    n_l = L // tl
    nc = C // tc
    base = (ATTN_WIDTH + 2 * KV_WIDTH) // tc
    r8 = tl // SUBLANES
    n8 = T // SUBLANES

    def main(k):
        return pl.BlockSpec((tl, tc), lambda i, j: (i, base + k * nc + j))

    def prev(k):
        return pl.BlockSpec((SUBLANES, tc), lambda i, j: (jnp.maximum(i * r8 - 1, 0), base + k * nc + j))

    def nxt(k):
        return pl.BlockSpec((SUBLANES, tc), lambda i, j: (jnp.minimum((i + 1) * r8, n8 - 1), base + k * nc + j))

    def wsp(k):
        return pl.BlockSpec((SHORT_CONV, tc), lambda i, j: (0, k * nc + j))

    def bsp(k):
        return pl.BlockSpec((1, tc), lambda i, j: (0, k * nc + j))

    cb = conv_b.reshape(1, 3 * C)
    kern = functools.partial(_hyena_pre_kernel, n_l=n_l)
    return pl.pallas_call(
        kern,
        grid=(T // tl, nc),
        in_specs=[main(0), main(1), main(2), prev(0), prev(1), prev(2), nxt(0), nxt(1), nxt(2),
                  wsp(0), wsp(1), wsp(2), bsp(0), bsp(1), bsp(2)],
        out_specs=[pl.BlockSpec((tl, tc), lambda i, j: (i, j)),
                   pl.BlockSpec((tl, tc), lambda i, j: (i, j))],
        out_shape=[jax.ShapeDtypeStruct((T, C), F32), jax.ShapeDtypeStruct((T, C), F32)],
        compiler_params=_params("parallel", "parallel"),
        name="hyena_pre",
    )(proj, proj, proj, proj, proj, proj, proj, proj, proj,
      conv_w, conv_w, conv_w, cb, cb, cb)


def _filter_kernel(z_ref, t_ref, w1_ref, b1_ref, w2_ref, b2_ref, w3_ref, b3_ref, fr_ref,
                   wo_ref, dec_ref, wob_ref, decb_ref, o_ref, *, n_half):
    i = pl.program_id(0)
    hi = lax.Precision.HIGHEST
    fr = fr_ref[...]
    h = jnp.sin(fr * (jnp.dot(z_ref[...], w1_ref[...], precision=hi, preferred_element_type=F32) + b1_ref[...]))
    h = jnp.sin(fr * (jnp.dot(h, w2_ref[...], precision=hi, preferred_element_type=F32) + b2_ref[...]))
    h = jnp.sin(fr * (jnp.dot(h, w3_ref[...], precision=hi, preferred_element_type=F32) + b3_ref[...]))
    taps = jnp.dot(h, wo_ref[...], precision=hi, preferred_element_type=F32)
    o_ref[...] = taps * jnp.exp(-t_ref[...] * jnp.abs(dec_ref[...]))
    first_row = lax.broadcasted_iota(jnp.int32, (SUBLANES, o_ref.shape[1]), 0) == 0

    @pl.when(i == 0)
    def _():
        hb = jnp.dot(h[:SUBLANES], wob_ref[...], precision=hi, preferred_element_type=F32)
        hb = hb * jnp.exp(-t_ref[0:SUBLANES, :] * jnp.abs(decb_ref[...]))
        o_ref[0:SUBLANES, :] = o_ref[0:SUBLANES, :] + jnp.where(first_row, hb, 0.0)

    @pl.when(i == n_half)
    def _():
        o_ref[0:SUBLANES, :] = jnp.where(first_row, 0.0, o_ref[0:SUBLANES, :])


def filter_features(L):
    t = jnp.linspace(0.0, 1.0, L, dtype=F32)[:, None]
    w = 2.0 * math.pi * jnp.arange(L, dtype=F32) / L
    bands = jnp.linspace(1e-4, FILTER_BANDS - 1, FILTER_BANDS, dtype=F32)
    ang = w[:, None] * bands[None, :]
    z = jnp.concatenate([t, jnp.cos(ang), -jnp.sin(ang)], axis=-1)
    return z, t


def long_conv_kernel_rows(L, w1, b1, w2, b2, w3, b3, freq, w_out, decay, tl=512, tn=1024):
    z, t = filter_features(L)
    mirror = lambda a: jnp.concatenate([a, a[:1], a[:0:-1]], axis=0)
    z, t = mirror(z), mirror(t)
    emb, hid = w1.shape
    emb_pad = -(-emb // LANES) * LANES
    z = jnp.pad(z, ((0, 0), (0, emb_pad - emb)))
    w1 = jnp.pad(w1, ((0, emb_pad - emb), (0, 0)))
    C = w_out.shape[1] // 2
    tl, tn = _pick(L, tl), _pick(C, tn)
    n_half, n_c = L // tl, C // tn
    row = lambda a: a.reshape(1, -1)
    const = lambda shape: pl.BlockSpec(shape, lambda i, j: (0, 0))
    side = lambda i, j: (0, jnp.where(i < n_half, j, n_c + j))
    back = lambda i, j: (0, n_c + j)
    return pl.pallas_call(
        functools.partial(_filter_kernel, n_half=n_half),
        grid=(2 * n_half, n_c),
        in_specs=[pl.BlockSpec((tl, emb_pad), lambda i, j: (i, 0)),
                  pl.BlockSpec((tl, 1), lambda i, j: (i, 0)),
                  const((emb_pad, hid)), const((1, hid)),
                  const((hid, hid)), const((1, hid)),
                  const((hid, hid)), const((1, hid)),
                  const((1, hid)),
                  pl.BlockSpec((hid, tn), side), pl.BlockSpec((1, tn), side),
                  pl.BlockSpec((hid, tn), back), pl.BlockSpec((1, tn), back)],
        out_specs=pl.BlockSpec((tl, tn), lambda i, j: (i, j)),
        out_shape=jax.ShapeDtypeStruct((2 * L, C), F32),
        compiler_params=_params("parallel", "parallel"),
        name="implicit_filter",
    )(z, t, w1, row(b1), w2, row(b2), w3, row(b3), row(freq), w_out, row(decay), w_out, row(decay))


def _dft_consts(L):
    N = 2 * L
    N2 = FFT_N2
    N1 = N // N2
    H = N1 // 2
    k1 = np.arange(N1)
    f1 = np.exp(-2j * np.pi * np.outer(k1, k1) / N1)
    fh = f1[:, :H]
    m1_pair = np.block([[fh.real, -fh.imag], [fh.imag, fh.real]])
    m1_single = np.concatenate([fh.real, fh.imag], axis=0)
    m1_full = np.concatenate([f1.real, f1.imag], axis=0)
    n2 = np.arange(N2)
    f2 = np.exp(-2j * np.pi * np.outer(n2, n2) / N2)
    tw = np.exp(-2j * np.pi * np.outer(k1, n2) / N)
    c = lambda a: jnp.asarray(a, F32)
    return dict(N=N, N1=N1, N2=N2, H=H,
                m1_pair=c(m1_pair), m1_single=c(m1_single), m1_full=c(m1_full),
                m3_pair=c(m1_pair.T / N), m3_single=c(m1_single.T / N),
                f2r=c(f2.real), f2i=c(f2.imag),
                twr=c(tw.real).reshape(N1, 1, N2), twi=c(tw.imag).reshape(N1, 1, N2))


def _dft_stage1_kernel(m_ref, x_ref, o_ref):
    rows = m_ref.shape[1]
    tc = x_ref.shape[-1]
    m = m_ref[...].astype(BF16)
    x = pltpu.einshape("mnc->nmc", x_ref[...].reshape(rows, SUBLANES, tc))
    r = jnp.stack([jnp.dot(m, x[s].astype(BF16), preferred_element_type=F32) for s in range(SUBLANES)], axis=0)
    o_ref[...] = pltpu.einshape("nmc->mnc", r).reshape(o_ref.shape)


def dft_stage1(x4, m1, tc=256):
    S, R, N2, C = x4.shape
    N1 = m1.shape[0] // 2
    tc = _pick(C, tc)
    return pl.pallas_call(
        _dft_stage1_kernel,
        grid=(N2 // SUBLANES, C // tc),
        in_specs=[pl.BlockSpec(m1.shape, lambda j, c: (0, 0)),
                  pl.BlockSpec((S, R, SUBLANES, tc), lambda j, c: (0, 0, j, c))],
        out_specs=pl.BlockSpec((2, N1, SUBLANES, tc), lambda j, c: (0, 0, j, c)),
        out_shape=jax.ShapeDtypeStruct((2, N1, N2, C), F32),
        compiler_params=_params("parallel", "parallel"),
        name="dft_stage1",
    )(m1, x4)


def _stage2_matrix(f2r_ref, f2i_ref, twr_ref, twi_ref):
    f2r, f2i = f2r_ref[...], f2i_ref[...]
    twr, twi = twr_ref[...], twi_ref[...]
    er = f2r * twr - f2i * twi
    ei = f2r * twi + f2i * twr
    return jnp.concatenate([jnp.concatenate([er, -ei], axis=1),
                            jnp.concatenate([ei, er], axis=1)], axis=0)


def _dft_stage2_kernel(f2r_ref, f2i_ref, twr_ref, twi_ref, a_ref, o_ref):
    n2, c = a_ref.shape[1], a_ref.shape[2]
    m = _stage2_matrix(f2r_ref, f2i_ref, twr_ref, twi_ref).astype(BF16)
    a = a_ref[...].reshape(2 * n2, c).astype(BF16)
    o_ref[...] = jnp.dot(m, a, preferred_element_type=F32).reshape(o_ref.shape)


def _spec_blocks(N1, N2, C):
    mat = pl.BlockSpec((N2, N2), lambda i: (0, 0))
    tw = pl.BlockSpec((None, 1, N2), lambda i: (i, 0, 0))
    dat = pl.BlockSpec((2, None, N2, C), lambda i: (0, i, 0, 0))
    return mat, tw, dat


def dft_stage2(a, cst):
    _, N1, N2, C = a.shape
    mat, tw, dat = _spec_blocks(N1, N2, C)
    return pl.pallas_call(
        _dft_stage2_kernel,
        grid=(N1,),
        in_specs=[mat, mat, tw, tw, dat],
        out_specs=dat,
        out_shape=jax.ShapeDtypeStruct(a.shape, F32),
        compiler_params=_params("parallel"),
        name="dft_stage2",
    )(cst["f2r"], cst["f2i"], cst["twr"], cst["twi"], a)


def _spectral_kernel(f2r_ref, f2i_ref, twr_ref, twi_ref, a_ref, k_ref, o_ref):
    n2, c = a_ref.shape[1], a_ref.shape[2]
    m = _stage2_matrix(f2r_ref, f2i_ref, twr_ref, twi_ref)
    a = a_ref[...].reshape(2 * n2, c).astype(BF16)
    x = jnp.dot(m.astype(BF16), a, preferred_element_type=F32)
    xr, xi = x[:n2], x[n2:]
    kr, ki = k_ref[0], k_ref[1]
    y = jnp.concatenate([xr * kr - xi * ki, xr * ki + xi * kr], axis=0).astype(BF16)
    o_ref[...] = jnp.dot(m.T.astype(BF16), y, preferred_element_type=F32).reshape(o_ref.shape)


def spectral_multiply(a, kspec, cst):
    _, N1, N2, C = a.shape
    mat, tw, dat = _spec_blocks(N1, N2, C)
    return pl.pallas_call(
        _spectral_kernel,
        grid=(N1,),
        in_specs=[mat, mat, tw, tw, dat, dat],
        out_specs=dat,
        out_shape=jax.ShapeDtypeStruct(a.shape, F32),
        compiler_params=_params("parallel"),
        name="spectral_multiply",
    )(cst["f2r"], cst["f2i"], cst["twr"], cst["twi"], a, kspec)


def _dft_final_kernel(m_ref, b_ref, u_ref, x0_ref, bias_ref, o_ref):
    rows = m_ref.shape[1]
    c = b_ref.shape[-1]
    m = m_ref[...].astype(BF16)
    b = pltpu.einshape("mnc->nmc", b_ref[...].reshape(rows, SUBLANES, c))
    y = jnp.stack([jnp.dot(m, b[s].astype(BF16), preferred_element_type=F32) for s in range(SUBLANES)], axis=0)
    y = pltpu.einshape("nmc->mnc", y).reshape(o_ref.shape)
    o_ref[...] = (y + u_ref[...] * bias_ref[...]) * x0_ref[...]


def dft_final(b, m3, u4, x04, bias, tc=256):
    S, R, N2, C = u4.shape
    N1 = b.shape[1]
    tc = _pick(C, tc)
    seq = pl.BlockSpec((S, R, SUBLANES, tc), lambda j, c: (0, 0, j, c))
    return pl.pallas_call(
        _dft_final_kernel,
        grid=(N2 // SUBLANES, C // tc),
        in_specs=[pl.BlockSpec(m3.shape, lambda j, c: (0, 0)),
                  pl.BlockSpec((2, N1, SUBLANES, tc), lambda j, c: (0, 0, j, c)),
                  seq, seq,
                  pl.BlockSpec((1, tc), lambda j, c: (0, c))],
        out_specs=seq,
        out_shape=jax.ShapeDtypeStruct(u4.shape, F32),
        compiler_params=_params("parallel", "parallel"),
        name="dft_final",
    )(m3, b, u4, x04, bias.reshape(1, C))


def long_conv_gate(u, x0, kern, bias, n_pair):
    B, L, C = u.shape
    cst = _dft_consts(L)
    N1, N2, H = cst["N1"], cst["N2"], cst["H"]
    kspec = dft_stage2(dft_stage1(kern.reshape(1, N1, N2, C), cst["m1_full"]), cst)
    outs = []
    for lo, hi in ((0, n_pair), (n_pair, B)):
        if hi == lo:
            continue
        pair = (hi - lo) == 2
        u4 = u[lo:hi].reshape(hi - lo, H, N2, C)
        x04 = x0[lo:hi].reshape(hi - lo, H, N2, C)
        a = dft_stage1(u4, cst["m1_pair"] if pair else cst["m1_single"])
        bsp = spectral_multiply(a, kspec, cst)
        y = dft_final(bsp, cst["m3_pair"] if pair else cst["m3_single"], u4, x04, bias)
        outs.append(y.reshape(hi - lo, L, C))
    return jnp.concatenate(outs, axis=0) if len(outs) > 1 else outs[0]


def _pack_bf16_pairs(h):
    half = h.shape[1] // 2
    bits = lax.bitcast_convert_type(h.astype(BF16).astype(F32), jnp.uint32)
    return bits[:, :half] | (bits[:, half:] >> 16)


def _unpack_bf16_pairs(w):
    hi = lax.bitcast_convert_type(w & jnp.uint32(0xFFFF0000), F32).astype(BF16)
    lo = lax.bitcast_convert_type(w << 16, F32).astype(BF16)
    return hi, lo


def _post_mixer_kernel(xa_ref, xb_ref, a_ref, hy_ref, ga_ref, gh_ref, wo_ref, gf_ref, wr_ref, br_ref,
                       x1_ref, h_ref, idx_ref, gate_ref, *, n_a):
    merged = jnp.concatenate([_rms(a_ref[...], ga_ref[...]), _rms(hy_ref[...], gh_ref[...])], axis=-1)
    x = jnp.where(pl.program_id(0) < n_a, xa_ref[...], xb_ref[...])
    x1 = x + jnp.dot(merged.astype(BF16), wo_ref[...], preferred_element_type=F32)
    x1_ref[...] = x1
    h = _rms(x1, gf_ref[...])
    h_ref[...] = _pack_bf16_pairs(h)
    logits = lax.dot_general(wr_ref[...], h, (((1,), (1,)), ((), ())),
                             precision=lax.Precision.HIGHEST, preferred_element_type=F32) + br_ref[...]
    e_iota = lax.broadcasted_iota(jnp.int32, logits.shape, 0).astype(F32)
    vals, idxs = [], []
    cur = logits
    for _ in range(TOP_K):
        m = jnp.max(cur, axis=0, keepdims=True)
        sel = jnp.min(jnp.where(cur == m, e_iota, float(N_EXPERTS)), axis=0, keepdims=True)
        vals.append(m)
        idxs.append(sel)
        cur = jnp.where(e_iota == sel, -jnp.inf, cur)
    ex = [jnp.exp(v - vals[0]) for v in vals]
    den = ex[0] + ex[1] + ex[2] + ex[3]
    idx_ref[...] = jnp.concatenate(idxs, axis=0).astype(jnp.int32)
    gate_ref[...] = jnp.concatenate([e / den for e in ex], axis=0)


def post_mixer(xa, xb, attn, hy, g_attn, g_hy, w_out, g_ffn, w_router_t, b_router, tm=512):
    Ta, D = xa.shape
    T = Ta + xb.shape[0]
    Wa, Wh = attn.shape[1], hy.shape[1]
    tm = _pick(math.gcd(Ta, xb.shape[0]), tm)
    n_a = Ta // tm
    spec_a, spec_b = _row_tiles_of_two(tm, D, n_a)
    const = lambda shape: pl.BlockSpec(shape, lambda i: (0, 0))
    rows = lambda w: pl.BlockSpec((tm, w), lambda i: (i, 0))
    cols = pl.BlockSpec((TOP_K, tm), lambda i: (0, i))
    return pl.pallas_call(
        functools.partial(_post_mixer_kernel, n_a=n_a),
        grid=(T // tm,),
        in_specs=[spec_a, spec_b, rows(Wa), rows(Wh), const((1, Wa)), const((1, Wh)), const((Wa + Wh, D)),
                  const((1, D)), const((N_EXPERTS, D)), const((N_EXPERTS, 1))],
        out_specs=[rows(D), rows(D // 2), cols, cols],
        out_shape=[jax.ShapeDtypeStruct((T, D), F32), jax.ShapeDtypeStruct((T, D // 2), jnp.uint32),
                   jax.ShapeDtypeStruct((TOP_K, T), jnp.int32), jax.ShapeDtypeStruct((TOP_K, T), F32)],
        compiler_params=_params("parallel"),
        name="post_mixer",
    )(xa, xb, attn, hy, g_attn.reshape(1, Wa), g_hy.reshape(1, Wh), w_out, g_ffn.reshape(1, D),
      w_router_t, b_router.reshape(N_EXPERTS, 1))


def route(top_idx, top_gate, tile):
    K, T = top_idx.shape
    n_assign = K * T
    e_flat = top_idx.reshape(-1)
    pos = jnp.arange(n_assign, dtype=jnp.int32)
    e_sorted, order, gate_sorted = lax.sort((e_flat, pos, top_gate.reshape(-1)), num_keys=1, is_stable=True)
    tok_sorted = order % T
    counts = jnp.sum((e_flat[None, :] == jnp.arange(N_EXPERTS, dtype=jnp.int32)[:, None]).astype(jnp.int32), axis=1)
    start = jnp.cumsum(counts) - counts
    padded = (counts + tile - 1) // tile * tile
    pend = jnp.cumsum(padded)
    pstart = pend - padded
    n_blocks = n_assign // tile + N_EXPERTS
    n_slots = n_blocks * tile
    block_start = jnp.arange(n_blocks, dtype=jnp.int32) * tile
    n_used = (pend[-1] // tile).astype(jnp.int32)
    block_expert = jnp.minimum(jnp.sum(block_start[:, None] >= pend[None, :], axis=1), N_EXPERTS - 1)
    last_expert = block_expert[jnp.maximum(n_used - 1, 0)]
    block_expert = jnp.where(jnp.arange(n_blocks) < n_used, block_expert, last_expert).astype(jnp.int32)
    slot = jnp.arange(n_slots, dtype=jnp.int32)
    slot_e = jnp.repeat(block_expert, tile)
    within = slot - pstart[slot_e]
    valid = jnp.logical_and(within < counts[slot_e], slot < pend[-1])
    src = jnp.clip(start[slot_e] + within, 0, n_assign - 1)
    slot_tok = jnp.where(valid, tok_sorted[src], 0)
    slot_gate = jnp.where(valid, gate_sorted[src], 0.0)
    slot_sorted = pstart[e_sorted] + (pos - start[e_sorted])
    _, slot_of = lax.sort((order, slot_sorted), num_keys=1)
    return slot_tok, slot_gate, slot_of.reshape(K, T), block_expert, n_used.reshape(1)


def _row_copy(src_ref, src_row, dst_ref, dst_row, sem):
    return pltpu.make_async_copy(src_ref.at[pl.ds(src_row, 1), :], dst_ref.at[pl.ds(dst_row, 1), :], sem)


DMA_ISSUE_UNROLL = 8


def _gather_kernel(nrows_ref, idx_ref, src_ref, dst_ref, sem, *, rows):
    live = pl.program_id(0) * rows < nrows_ref[0]

    @pl.when(live)
    def _():
        def issue(r, c):
            _row_copy(src_ref, idx_ref[0, 0, r], dst_ref, r, sem).start()
            return c

        lax.fori_loop(0, rows, issue, 0, unroll=DMA_ISSUE_UNROLL)
        pltpu.make_async_copy(src_ref.at[pl.ds(0, rows), :], dst_ref, sem).wait()

    @pl.when(jnp.logical_not(live))
    def _():
        dst_ref[...] = jnp.zeros(dst_ref.shape, dst_ref.dtype)


def gather_rows(src, idx, n_live_rows, rows=512):
    T, W = src.shape
    n = idx.shape[0]
    rows = _pick(n, rows)
    return pl.pallas_call(
        functools.partial(_gather_kernel, rows=rows),
        grid=(n // rows,),
        in_specs=[pl.BlockSpec(memory_space=pltpu.SMEM),
                  pl.BlockSpec((1, 1, rows), lambda i: (i, 0, 0), memory_space=pltpu.SMEM),
                  pl.BlockSpec(memory_space=pl.ANY)],
        out_specs=pl.BlockSpec((rows, W), lambda i: (i, 0)),
        out_shape=jax.ShapeDtypeStruct((n, W), src.dtype),
        scratch_shapes=[pltpu.SemaphoreType.DMA(())],
        compiler_params=_params("arbitrary"),
        name="gather_rows",
    )(n_live_rows, idx.reshape(n // rows, 1, rows), src)


def _expert_kernel(be_ref, nu_ref, x_ref, wg_ref, wl_ref, bg_ref, bl_ref, wd_ref, bd_ref, g_ref,
                   o_ref, xs_sc, *, n_f):
    b = pl.program_id(0)
    f = pl.program_id(1)
    half = x_ref.shape[1]
    used = b < nu_ref[0]

    @pl.when(jnp.logical_and(used, f == 0))
    def _():
        hi, lo = _unpack_bf16_pairs(x_ref[...])
        xs_sc[:, :half] = hi
        xs_sc[:, half:] = lo

    @pl.when(used)
    def _():
        x = xs_sc[...]
        glu = jnp.dot(x, wg_ref[0], preferred_element_type=F32) + bg_ref[0]
        lin = jnp.dot(x, wl_ref[0], preferred_element_type=F32) + bl_ref[0]
        glu = jnp.minimum(glu, SWIGLU_LIMIT)
        lin = jnp.clip(lin, -SWIGLU_LIMIT, SWIGLU_LIMIT)
        y = glu * jax.nn.sigmoid(SWIGLU_ALPHA * glu) * (lin + 1.0)
        part = jnp.dot(y.astype(BF16), wd_ref[0], preferred_element_type=F32)

        @pl.when(f == 0)
        def _():
            o_ref[...] = part

        @pl.when(f > 0)
        def _():
            o_ref[...] += part

        @pl.when(f == n_f - 1)
        def _():
            g = g_ref[...]
            for c in range(o_ref.shape[1] // LANES):
                sl = slice(c * LANES, (c + 1) * LANES)
                o_ref[:, sl] = (o_ref[:, sl] + bd_ref[0, :, sl]) * g

    @pl.when(jnp.logical_and(jnp.logical_not(used), f == n_f - 1))
    def _():
        o_ref[...] = jnp.zeros(o_ref.shape, o_ref.dtype)


def expert_mlp(x_slots, block_expert, n_used, w_up, b_up, w_down, b_down, slot_gate,
               tile=EXPERT_TILE, tf=FF_TILE):
    n_slots, half = x_slots.shape
    D = 2 * half
    E, _, two_ff = w_up.shape
    d_ff = two_ff // 2
    tf = _pick(d_ff, tf)
    n_f = d_ff // tf
    n_blocks = n_slots // tile
    gate_rep = jnp.broadcast_to(slot_gate[:, None], (n_slots, LANES))

    def blk(b, nu):
        return jnp.minimum(b, jnp.maximum(nu[0] - 1, 0))

    def ff(b, f, nu):
        return jnp.where(b < nu[0], f, n_f - 1)

    grid_spec = pltpu.PrefetchScalarGridSpec(
        num_scalar_prefetch=2,
        grid=(n_blocks, n_f),
        in_specs=[pl.BlockSpec((tile, half), lambda b, f, be, nu: (blk(b, nu), 0)),
                  pl.BlockSpec((1, D, tf), lambda b, f, be, nu: (be[b], 0, ff(b, f, nu))),
                  pl.BlockSpec((1, D, tf), lambda b, f, be, nu: (be[b], 0, n_f + ff(b, f, nu))),
                  pl.BlockSpec((1, 1, tf), lambda b, f, be, nu: (be[b], 0, ff(b, f, nu))),
                  pl.BlockSpec((1, 1, tf), lambda b, f, be, nu: (be[b], 0, n_f + ff(b, f, nu))),
                  pl.BlockSpec((1, tf, D), lambda b, f, be, nu: (be[b], ff(b, f, nu), 0)),
                  pl.BlockSpec((1, 1, D), lambda b, f, be, nu: (be[b], 0, 0)),
                  pl.BlockSpec((tile, LANES), lambda b, f, be, nu: (blk(b, nu), 0))],
        out_specs=pl.BlockSpec((tile, D), lambda b, f, be, nu: (b, 0)),
        scratch_shapes=[pltpu.VMEM((tile, D), BF16)])
    return pl.pallas_call(
        functools.partial(_expert_kernel, n_f=n_f),
        grid_spec=grid_spec,
        out_shape=jax.ShapeDtypeStruct((n_slots, D), F32),
        compiler_params=_params("arbitrary", "arbitrary"),
        name="expert_mlp",
    )(block_expert, n_used, x_slots, w_up, w_up, b_up.reshape(E, 1, two_ff), b_up.reshape(E, 1, two_ff),
      w_down, b_down.reshape(E, 1, D), gate_rep)


def _final_kernel(slot_ref, y_hbm, x1_ref, pa_ref, pb_ref, gp_ref, wg_ref, wp_ref, gf_ref, oa_ref, ob_ref,
                  rows_sc, sem, *, tm, n_a):
    def issue(r, c):
        for k in range(TOP_K):
            _row_copy(y_hbm, slot_ref[0, k, r], rows_sc.at[k], r, sem).start()
        return c

    lax.fori_loop(0, tm, issue, 0, unroll=DMA_ISSUE_UNROLL // TOP_K)
    for k in range(TOP_K):
        pltpu.make_async_copy(y_hbm.at[pl.ds(0, tm), :], rows_sc.at[k], sem).wait()

    x2 = x1_ref[...] + ((rows_sc[0] + rows_sc[1]) + (rows_sc[2] + rows_sc[3]))
    gate = jax.nn.sigmoid(jnp.dot(_rms(x2, gp_ref[...]).astype(BF16), wg_ref[...],
                                  preferred_element_type=F32))
    from_a = pl.program_id(0) < n_a
    p = jnp.where(from_a, pa_ref[...], pb_ref[...])
    x3 = x2 + gate * jnp.dot(p.astype(BF16), wp_ref[...], preferred_element_type=F32)
    y = _rms(x3, gf_ref[...])

    @pl.when(from_a)
    def _():
        oa_ref[...] = y

    @pl.when(jnp.logical_not(from_a))
    def _():
        ob_ref[...] = y


def combine_ple_final(y_slots, slot_of, x1, pa, pb, g_ple, w_gate, w_proj, g_final, tm=256):
    T, D = x1.shape
    Ta, P = pa.shape
    Tb = pb.shape[0]
    tm = _pick(math.gcd(Ta, Tb), tm)
    n_a = Ta // tm
    pa_spec, pb_spec = _row_tiles_of_two(tm, P, n_a)
    oa_spec, ob_spec = _row_tiles_of_two(tm, D, n_a)
    slots = slot_of.reshape(TOP_K, T // tm, tm).transpose(1, 0, 2)
    const = lambda shape: pl.BlockSpec(shape, lambda i: (0, 0))
    return pl.pallas_call(
        functools.partial(_final_kernel, tm=tm, n_a=n_a),
        grid=(T // tm,),
        in_specs=[pl.BlockSpec((1, TOP_K, tm), lambda i: (i, 0, 0), memory_space=pltpu.SMEM),
                  pl.BlockSpec(memory_space=pl.ANY),
                  pl.BlockSpec((tm, D), lambda i: (i, 0)),
                  pa_spec, pb_spec,
                  const((1, D)), const((D, D)), const((P, D)), const((1, D))],
        out_specs=[oa_spec, ob_spec],
        out_shape=[jax.ShapeDtypeStruct((Ta, D), F32), jax.ShapeDtypeStruct((Tb, D), F32)],
        scratch_shapes=[pltpu.VMEM((TOP_K, tm, D), F32),
                        pltpu.SemaphoreType.DMA(())],
        compiler_params=_params("arbitrary"),
        name="combine_ple_final",
    )(slots, y_slots, x1, pa, pb, g_ple.reshape(1, D), w_gate, w_proj, g_final.reshape(1, D))


def kernel(x_prompt, x_sample, p_prompt, p_sample, g_mix, w_in, q_gain, k_gain, conv_w, conv_b, filt_w1, filt_b1, filt_w2, filt_b2, filt_w3, filt_b3, filt_freq, filt_w_out, filt_decay, hyena_bias, g_attn_out, g_hyena_out, w_out, g_ffn, w_router, b_router, w_up, b_up, w_down, b_down, g_ple, w_ple_gate, w_ple_proj, g_final):
    depth = g_mix.shape[0]
    bp, L, D = x_prompt.shape
    bs = x_sample.shape[0]
    B = bp + bs
    T = B * L
    C = hyena_bias.shape[1]
    assert depth == 1
    xa, xb = x_prompt.reshape(bp * L, D), x_sample.reshape(bs * L, D)
    cos_t, sin_t = rope_tables(L)
    n_pair = bp if bp == 2 else 0

    for i in range(depth):
        proj = norm_matmul(xa, xb, g_mix[i], w_in[i].astype(BF16))
        q, k, v = qkv_prep(proj, q_gain[i], k_gain[i], cos_t, sin_t, L)
        attn = flash_attention(q.reshape(B, L, ATTN_WIDTH), k.reshape(B, L, KV_WIDTH),
                               v.reshape(B, L, 2 * KV_WIDTH)).reshape(T, ATTN_WIDTH)
        u, x0 = hyena_pre(proj, conv_w[i], conv_b[i], L)
        kern = long_conv_kernel_rows(L, filt_w1[i], filt_b1[i], filt_w2[i], filt_b2[i], filt_w3[i], filt_b3[i],
                                     filt_freq[i], filt_w_out[i], filt_decay[i])
        hy = long_conv_gate(u.reshape(B, L, C), x0.reshape(B, L, C), kern, hyena_bias[i], n_pair).reshape(T, C)
        x1, h, top_idx, top_gate = post_mixer(xa, xb, attn, hy, g_attn_out[i], g_hyena_out[i],
                                              w_out[i].astype(BF16), g_ffn[i], w_router[i].T, b_router[i])
        slot_tok, slot_gate, slot_of, block_expert, n_used = route(top_idx, top_gate, EXPERT_TILE)
        x_slots = gather_rows(h, slot_tok, n_used * EXPERT_TILE)
        y_slots = expert_mlp(x_slots, block_expert, n_used, w_up[i].astype(BF16), b_up[i],
                             w_down[i].astype(BF16), b_down[i], slot_gate)
        pa = p_prompt[i].reshape(bp * L, -1)
        pb = p_sample[i].reshape(bs * L, -1)
        ya, yb = combine_ple_final(y_slots, slot_of, x1, pa, pb, g_ple[i],
                                   w_ple_gate[i].astype(BF16), w_ple_proj[i].astype(BF16), g_final)
    return (ya.reshape(bp, L, D), yb.reshape(bs, L, D))
```

```python
import functools
import math

import numpy as np
import jax
import jax.numpy as jnp
from jax import lax
from jax.experimental import pallas as pl
from jax.experimental.pallas import tpu as pltpu

F32 = jnp.float32
BF16 = jnp.bfloat16

HEAD_DIM = 128
N_Q_HEADS = 8
N_KV_HEADS = 2
Q_PER_KV = N_Q_HEADS // N_KV_HEADS
ATTN_WIDTH = N_Q_HEADS * HEAD_DIM
KV_WIDTH = N_KV_HEADS * HEAD_DIM
GRID_W = 64
ROPE_AXIS_DIM = HEAD_DIM // 2
ROPE_THETA = 10000.0
SHORT_CONV = 3
FILTER_EMB = 33
FILTER_BANDS = (FILTER_EMB - 1) // 2
N_EXPERTS = 32
TOP_K = 4
SWIGLU_ALPHA = 1.702
SWIGLU_LIMIT = 7.0
EPS = 1e-6
LOG2E = 1.4426950408889634

LANES = 128
SUBLANES = 8
VMEM_LIMIT_BYTES = 56 * 1024 * 1024

FFT_N2 = 128
EXPERT_TILE = 512
FF_TILE = 1024


def _params(*sem):
    return pltpu.CompilerParams(dimension_semantics=sem, vmem_limit_bytes=VMEM_LIMIT_BYTES)


def _pick(n, want):
    t = min(n, want)
    while n % t:
        t //= 2
    return t


def _rms(x, g):
    return x * lax.rsqrt(jnp.mean(x * x, axis=-1, keepdims=True) + EPS) * g


def _row_tiles_of_two(tm, width, n_a):
    a = pl.BlockSpec((tm, width), lambda i, *_: (jnp.minimum(i, n_a - 1), 0))
    b = pl.BlockSpec((tm, width), lambda i, *_: (jnp.maximum(i - n_a, 0), 0))
    return a, b


def _norm_matmul_kernel(xa_ref, xb_ref, g_ref, w_ref, o_ref, h_sc, *, n_a):
    @pl.when(pl.program_id(1) == 0)
    def _():
        x = jnp.where(pl.program_id(0) < n_a, xa_ref[...], xb_ref[...])
        h_sc[...] = _rms(x, g_ref[...]).astype(BF16)

    o_ref[...] = jnp.dot(h_sc[...], w_ref[...], preferred_element_type=F32)


def norm_matmul(xa, xb, g, w, tm=1024, tn=512):
    Ta, D = xa.shape
    Tb = xb.shape[0]
    N = w.shape[1]
    tm, tn = _pick(math.gcd(Ta, Tb), tm), _pick(N, tn)
    n_a = Ta // tm
    spec_a, spec_b = _row_tiles_of_two(tm, D, n_a)
    return pl.pallas_call(
        functools.partial(_norm_matmul_kernel, n_a=n_a),
        grid=((Ta + Tb) // tm, N // tn),
        in_specs=[spec_a, spec_b,
                  pl.BlockSpec((1, D), lambda i, j: (0, 0)),
                  pl.BlockSpec((D, tn), lambda i, j: (0, j))],
        out_specs=pl.BlockSpec((tm, tn), lambda i, j: (i, j)),
        out_shape=jax.ShapeDtypeStruct((Ta + Tb, N), F32),
        scratch_shapes=[pltpu.VMEM((tm, D), BF16)],
        compiler_params=_params("parallel", "arbitrary"),
        name="norm_matmul",
    )(xa, xb, g.reshape(1, D), w)


def _qkv_prep_kernel(p_ref, qg_ref, kg_ref, c_ref, s_ref, q_ref, k_ref, v_ref):
    cos = c_ref[...]
    sin = s_ref[...]
    lane = lax.broadcasted_iota(jnp.int32, cos.shape, 1)
    low_half = (lane % ROPE_AXIS_DIM) < (ROPE_AXIS_DIM // 2)

    def norm_rope(x, g):
        y = _rms(x, g)
        fwd = pltpu.roll(y, ROPE_AXIS_DIM // 2, axis=1)
        bwd = pltpu.roll(y, HEAD_DIM - ROPE_AXIS_DIM // 2, axis=1)
        return y * cos + jnp.where(low_half, bwd, fwd) * sin

    q_scale = (HEAD_DIM ** -0.5) * LOG2E
    for h in range(N_Q_HEADS):
        sl = slice(h * HEAD_DIM, (h + 1) * HEAD_DIM)
        q_ref[:, sl] = (norm_rope(p_ref[:, sl], qg_ref[...]) * q_scale).astype(BF16)
    for h in range(N_KV_HEADS):
        src = slice(ATTN_WIDTH + h * HEAD_DIM, ATTN_WIDTH + (h + 1) * HEAD_DIM)
        dst = slice(h * HEAD_DIM, (h + 1) * HEAD_DIM)
        k_ref[:, dst] = norm_rope(p_ref[:, src], kg_ref[...]).astype(BF16)
    ones = jnp.ones((p_ref.shape[0], HEAD_DIM), BF16)
    for h in range(N_KV_HEADS):
        src = slice(ATTN_WIDTH + KV_WIDTH + h * HEAD_DIM, ATTN_WIDTH + KV_WIDTH + (h + 1) * HEAD_DIM)
        v_ref[:, 2 * h * HEAD_DIM:(2 * h + 1) * HEAD_DIM] = p_ref[:, src].astype(BF16)
        v_ref[:, (2 * h + 1) * HEAD_DIM:(2 * h + 2) * HEAD_DIM] = ones


def qkv_prep(proj, q_gain, k_gain, cos_t, sin_t, L, tl=512):
    T = proj.shape[0]
    tl = _pick(L, tl)
    nl = L // tl
    w_qkv = ATTN_WIDTH + 2 * KV_WIDTH
    return pl.pallas_call(
        _qkv_prep_kernel,
        grid=(T // tl,),
        in_specs=[pl.BlockSpec((tl, w_qkv), lambda i: (i, 0)),
                  pl.BlockSpec((1, HEAD_DIM), lambda i: (0, 0)),
                  pl.BlockSpec((1, HEAD_DIM), lambda i: (0, 0)),
                  pl.BlockSpec((tl, HEAD_DIM), lambda i: (i % nl, 0)),
                  pl.BlockSpec((tl, HEAD_DIM), lambda i: (i % nl, 0))],
        out_specs=[pl.BlockSpec((tl, ATTN_WIDTH), lambda i: (i, 0)),
                   pl.BlockSpec((tl, KV_WIDTH), lambda i: (i, 0)),
                   pl.BlockSpec((tl, 2 * KV_WIDTH), lambda i: (i, 0))],
        out_shape=[jax.ShapeDtypeStruct((T, ATTN_WIDTH), BF16),
                   jax.ShapeDtypeStruct((T, KV_WIDTH), BF16),
                   jax.ShapeDtypeStruct((T, 2 * KV_WIDTH), BF16)],
        compiler_params=_params("parallel"),
        name="qkv_prep",
    )(proj, q_gain.reshape(1, HEAD_DIM), k_gain.reshape(1, HEAD_DIM), cos_t, sin_t)


def rope_tables(L):
    rows = L // GRID_W
    row_idx = jnp.repeat(jnp.arange(rows, dtype=F32), GRID_W)
    col_idx = jnp.tile(jnp.arange(GRID_W, dtype=F32), rows)
    inv_freq = ROPE_THETA ** (-jnp.arange(0, ROPE_AXIS_DIM, 2, dtype=F32) / ROPE_AXIS_DIM)
    ar = row_idx[:, None] * inv_freq[None, :]
    ac = col_idx[:, None] * inv_freq[None, :]
    cos_t = jnp.concatenate([jnp.cos(ar), jnp.cos(ar), jnp.cos(ac), jnp.cos(ac)], axis=-1)
    sin_t = jnp.concatenate([-jnp.sin(ar), jnp.sin(ar), -jnp.sin(ac), jnp.sin(ac)], axis=-1)
    return cos_t, sin_t


def _attn_kernel(q_ref, k_ref, v_ref, o_ref, m_sc, acc_sc, *, tk, n_kv):
    m_sc[...] = jnp.full(m_sc.shape, -jnp.inf, F32)
    acc_sc[...] = jnp.zeros(acc_sc.shape, F32)

    def body(j, carry):
        off = pl.multiple_of(j * tk, tk)
        k = k_ref[0, pl.ds(off, tk), :]
        v = v_ref[0, pl.ds(off, tk), :]
        for g in range(Q_PER_KV):
            q = q_ref[0, :, g * HEAD_DIM:(g + 1) * HEAD_DIM]
            s = lax.dot_general(q, k, (((1,), (1,)), ((), ())), preferred_element_type=F32)
            m_prev = m_sc[g]
            m_new = jnp.maximum(m_prev, jnp.max(s, axis=-1, keepdims=True))
            alpha = jnp.exp2(m_prev - m_new)
            p = jnp.exp2(s - jnp.tile(m_new, (1, tk // LANES)))
            pv = jnp.dot(p.astype(BF16), v, preferred_element_type=F32)
            acc_sc[g] = jnp.tile(alpha, (1, 2 * HEAD_DIM // LANES)) * acc_sc[g] + pv
            m_sc[g] = m_new
        return carry

    lax.fori_loop(0, n_kv, body, 0)
    for g in range(Q_PER_KV):
        acc = acc_sc[g]
        o_ref[0, :, g * HEAD_DIM:(g + 1) * HEAD_DIM] = acc[:, :HEAD_DIM] / acc[:, HEAD_DIM:]


def flash_attention(q, k, v1, tq=2048, tk=256):
    B, L, _ = q.shape
    tq, tk = _pick(L, tq), _pick(L, tk)
    gw = Q_PER_KV * HEAD_DIM
    kern = functools.partial(_attn_kernel, tk=tk, n_kv=L // tk)
    return pl.pallas_call(
        kern,
        grid=(B, N_KV_HEADS, L // tq),
        in_specs=[pl.BlockSpec((1, tq, gw), lambda b, h, i: (b, i, h)),
                  pl.BlockSpec((1, L, HEAD_DIM), lambda b, h, i: (b, 0, h)),
                  pl.BlockSpec((1, L, 2 * HEAD_DIM), lambda b, h, i: (b, 0, h))],
        out_specs=pl.BlockSpec((1, tq, gw), lambda b, h, i: (b, i, h)),
        out_shape=jax.ShapeDtypeStruct((B, L, ATTN_WIDTH), F32),
        scratch_shapes=[pltpu.VMEM((Q_PER_KV, tq, LANES), F32),
                        pltpu.VMEM((Q_PER_KV, tq, 2 * HEAD_DIM), F32)],
        compiler_params=_params("parallel", "parallel", "arbitrary"),
        name="flash_attention",
    )(q, k, v1)


def _hyena_pre_kernel(x0_ref, x1_ref, v_ref, x0p_ref, x1p_ref, vp_ref, x0n_ref, x1n_ref, vn_ref,
                      w0_ref, w1_ref, w2_ref, b0_ref, b1_ref, b2_ref, u_ref, x0o_ref, *, n_l):
    i = pl.program_id(0) % n_l
    first = i == 0
    last = i == n_l - 1
    tl = x0_ref.shape[0]
    row = lax.broadcasted_iota(jnp.int32, x0_ref.shape, 0)

    def conv(x_ref, p_ref, n_ref, w_ref, b_ref):
        x = x_ref[...]
        prev_row = jnp.where(first, 0.0, p_ref[SUBLANES - 1:SUBLANES, :])
        next_row = jnp.where(last, 0.0, n_ref[0:1, :])
        xm = jnp.where(row == 0, prev_row, pltpu.roll(x, 1, axis=0))
        xp = jnp.where(row == tl - 1, next_row, pltpu.roll(x, tl - 1, axis=0))
        return xm * w_ref[0:1, :] + x * w_ref[1:2, :] + xp * w_ref[2:3, :] + b_ref[...]

    x0o_ref[...] = conv(x0_ref, x0p_ref, x0n_ref, w0_ref, b0_ref)
    u_ref[...] = conv(v_ref, vp_ref, vn_ref, w2_ref, b2_ref) * conv(x1_ref, x1p_ref, x1n_ref, w1_ref, b1_ref)


def hyena_pre(proj, conv_w, conv_b, L, tl=512, tc=512):
    T = proj.shape[0]
    C = conv_w.shape[1] // 3
    tl, tc = _pick(L, tl), _pick(C, tc)
    n_l = L // tl
    nc = C // tc
    base = (ATTN_WIDTH + 2 * KV_WIDTH) // tc
    r8 = tl // SUBLANES
    n8 = T // SUBLANES

    def main(k):
        return pl.BlockSpec((tl, tc), lambda i, j: (i, base + k * nc + j))

    def prev(k):
        return pl.BlockSpec((SUBLANES, tc), lambda i, j: (jnp.maximum(i * r8 - 1, 0), base + k * nc + j))

    def nxt(k):
        return pl.BlockSpec((SUBLANES, tc), lambda i, j: (jnp.minimum((i + 1) * r8, n8 - 1), base + k * nc + j))

    def wsp(k):
        return pl.BlockSpec((SHORT_CONV, tc), lambda i, j: (0, k * nc + j))

    def bsp(k):
        return pl.BlockSpec((1, tc), lambda i, j: (0, k * nc + j))

    cb = conv_b.reshape(1, 3 * C)
    kern = functools.partial(_hyena_pre_kernel, n_l=n_l)
    return pl.pallas_call(
        kern,
        grid=(T // tl, nc),
        in_specs=[main(0), main(1), main(2), prev(0), prev(1), prev(2), nxt(0), nxt(1), nxt(2),
                  wsp(0), wsp(1), wsp(2), bsp(0), bsp(1), bsp(2)],
        out_specs=[pl.BlockSpec((tl, tc), lambda i, j: (i, j)),
                   pl.BlockSpec((tl, tc), lambda i, j: (i, j))],
        out_shape=[jax.ShapeDtypeStruct((T, C), F32), jax.ShapeDtypeStruct((T, C), F32)],
        compiler_params=_params("parallel", "parallel"),
        name="hyena_pre",
    )(proj, proj, proj, proj, proj, proj, proj, proj, proj,
      conv_w, conv_w, conv_w, cb, cb, cb)


def _filter_kernel(z_ref, t_ref, w1_ref, b1_ref, w2_ref, b2_ref, w3_ref, b3_ref, fr_ref,
                   wo_ref, dec_ref, wob_ref, decb_ref, o_ref, *, n_half):
    i = pl.program_id(0)
    hi = lax.Precision.HIGHEST
    fr = fr_ref[...]
    h = jnp.sin(fr * (jnp.dot(z_ref[...], w1_ref[...], precision=hi, preferred_element_type=F32) + b1_ref[...]))
    h = jnp.sin(fr * (jnp.dot(h, w2_ref[...], precision=hi, preferred_element_type=F32) + b2_ref[...]))
    h = jnp.sin(fr * (jnp.dot(h, w3_ref[...], precision=hi, preferred_element_type=F32) + b3_ref[...]))
    taps = jnp.dot(h, wo_ref[...], precision=hi, preferred_element_type=F32)
    o_ref[...] = taps * jnp.exp(-t_ref[...] * jnp.abs(dec_ref[...]))
    first_row = lax.broadcasted_iota(jnp.int32, (SUBLANES, o_ref.shape[1]), 0) == 0

    @pl.when(i == 0)
    def _():
        hb = jnp.dot(h[:SUBLANES], wob_ref[...], precision=hi, preferred_element_type=F32)
        hb = hb * jnp.exp(-t_ref[0:SUBLANES, :] * jnp.abs(decb_ref[...]))
        o_ref[0:SUBLANES, :] = o_ref[0:SUBLANES, :] + jnp.where(first_row, hb, 0.0)

    @pl.when(i == n_half)
    def _():
        o_ref[0:SUBLANES, :] = jnp.where(first_row, 0.0, o_ref[0:SUBLANES, :])


def filter_features(L):
    t = jnp.linspace(0.0, 1.0, L, dtype=F32)[:, None]
    w = 2.0 * math.pi * jnp.arange(L, dtype=F32) / L
    bands = jnp.linspace(1e-4, FILTER_BANDS - 1, FILTER_BANDS, dtype=F32)
    ang = w[:, None] * bands[None, :]
    z = jnp.concatenate([t, jnp.cos(ang), -jnp.sin(ang)], axis=-1)
    return z, t


def long_conv_kernel_rows(L, w1, b1, w2, b2, w3, b3, freq, w_out, decay, tl=512, tn=1024):
    z, t = filter_features(L)
    mirror = lambda a: jnp.concatenate([a, a[:1], a[:0:-1]], axis=0)
    z, t = mirror(z), mirror(t)
    emb, hid = w1.shape
    emb_pad = -(-emb // LANES) * LANES
    z = jnp.pad(z, ((0, 0), (0, emb_pad - emb)))
    w1 = jnp.pad(w1, ((0, emb_pad - emb), (0, 0)))
    C = w_out.shape[1] // 2
    tl, tn = _pick(L, tl), _pick(C, tn)
    n_half, n_c = L // tl, C // tn
    row = lambda a: a.reshape(1, -1)
    const = lambda shape: pl.BlockSpec(shape, lambda i, j: (0, 0))
    side = lambda i, j: (0, jnp.where(i < n_half, j, n_c + j))
    back = lambda i, j: (0, n_c + j)
    return pl.pallas_call(
        functools.partial(_filter_kernel, n_half=n_half),
        grid=(2 * n_half, n_c),
        in_specs=[pl.BlockSpec((tl, emb_pad), lambda i, j: (i, 0)),
                  pl.BlockSpec((tl, 1), lambda i, j: (i, 0)),
                  const((emb_pad, hid)), const((1, hid)),
                  const((hid, hid)), const((1, hid)),
                  const((hid, hid)), const((1, hid)),
                  const((1, hid)),
                  pl.BlockSpec((hid, tn), side), pl.BlockSpec((1, tn), side),
                  pl.BlockSpec((hid, tn), back), pl.BlockSpec((1, tn), back)],
        out_specs=pl.BlockSpec((tl, tn), lambda i, j: (i, j)),
        out_shape=jax.ShapeDtypeStruct((2 * L, C), F32),
        compiler_params=_params("parallel", "parallel"),
        name="implicit_filter",
    )(z, t, w1, row(b1), w2, row(b2), w3, row(b3), row(freq), w_out, row(decay), w_out, row(decay))


def _dft_consts(L):
    N = 2 * L
    N2 = FFT_N2
    N1 = N // N2
    H = N1 // 2
    k1 = np.arange(N1)
    f1 = np.exp(-2j * np.pi * np.outer(k1, k1) / N1)
    fh = f1[:, :H]
    m1_pair = np.block([[fh.real, -fh.imag], [fh.imag, fh.real]])
    m1_single = np.concatenate([fh.real, fh.imag], axis=0)
    m1_full = np.concatenate([f1.real, f1.imag], axis=0)
    n2 = np.arange(N2)
    f2 = np.exp(-2j * np.pi * np.outer(n2, n2) / N2)
    tw = np.exp(-2j * np.pi * np.outer(k1, n2) / N)
    c = lambda a: jnp.asarray(a, F32)
    return dict(N=N, N1=N1, N2=N2, H=H,
                m1_pair=c(m1_pair), m1_single=c(m1_single), m1_full=c(m1_full),
                m3_pair=c(m1_pair.T / N), m3_single=c(m1_single.T / N),
                f2r=c(f2.real), f2i=c(f2.imag),
                twr=c(tw.real).reshape(N1, 1, N2), twi=c(tw.imag).reshape(N1, 1, N2))


def _dft_stage1_kernel(m_ref, x_ref, o_ref):
    rows = m_ref.shape[1]
    tc = x_ref.shape[-1]
    m = m_ref[...].astype(BF16)
    x = pltpu.einshape("mnc->nmc", x_ref[...].reshape(rows, SUBLANES, tc))
    r = jnp.stack([jnp.dot(m, x[s].astype(BF16), preferred_element_type=F32) for s in range(SUBLANES)], axis=0)
    o_ref[...] = pltpu.einshape("nmc->mnc", r).reshape(o_ref.shape)


def dft_stage1(x4, m1, lo=0, S=None, tc=256):
    B, R, N2, C = x4.shape
    S = B if S is None else S
    assert lo % S == 0 and m1.shape[1] == S * R
    N1 = m1.shape[0] // 2
    tc = _pick(C, tc)
    return pl.pallas_call(
        _dft_stage1_kernel,
        grid=(N2 // SUBLANES, C // tc),
        in_specs=[pl.BlockSpec(m1.shape, lambda j, c: (0, 0)),
                  pl.BlockSpec((S, R, SUBLANES, tc), lambda j, c: (lo // S, 0, j, c))],
        out_specs=pl.BlockSpec((2, N1, SUBLANES, tc), lambda j, c: (0, 0, j, c)),
        out_shape=jax.ShapeDtypeStruct((2, N1, N2, C), F32),
        compiler_params=_params("parallel", "parallel"),
        name="dft_stage1",
    )(m1, x4)


def _stage2_matrix(f2r_ref, f2i_ref, twr_ref, twi_ref):
    f2r, f2i = f2r_ref[...], f2i_ref[...]
    twr, twi = twr_ref[...], twi_ref[...]
    er = f2r * twr - f2i * twi
    ei = f2r * twi + f2i * twr
    return jnp.concatenate([jnp.concatenate([er, -ei], axis=1),
                            jnp.concatenate([ei, er], axis=1)], axis=0)


def _dft_stage2_kernel(f2r_ref, f2i_ref, twr_ref, twi_ref, a_ref, o_ref):
    n2, c = a_ref.shape[1], a_ref.shape[2]
    m = _stage2_matrix(f2r_ref, f2i_ref, twr_ref, twi_ref).astype(BF16)
    a = a_ref[...].reshape(2 * n2, c).astype(BF16)
    o_ref[...] = jnp.dot(m, a, preferred_element_type=F32).reshape(o_ref.shape)


def _spec_blocks(N1, N2, C):
    mat = pl.BlockSpec((N2, N2), lambda i: (0, 0))
    tw = pl.BlockSpec((None, 1, N2), lambda i: (i, 0, 0))
    dat = pl.BlockSpec((2, None, N2, C), lambda i: (0, i, 0, 0))
    return mat, tw, dat


def dft_stage2(a, cst):
    _, N1, N2, C = a.shape
    mat, tw, dat = _spec_blocks(N1, N2, C)
    return pl.pallas_call(
        _dft_stage2_kernel,
        grid=(N1,),
        in_specs=[mat, mat, tw, tw, dat],
        out_specs=dat,
        out_shape=jax.ShapeDtypeStruct(a.shape, F32),
        compiler_params=_params("parallel"),
        name="dft_stage2",
    )(cst["f2r"], cst["f2i"], cst["twr"], cst["twi"], a)


def _spectral_kernel(f2r_ref, f2i_ref, twr_ref, twi_ref, a_ref, k_ref, o_ref):
    n2, c = a_ref.shape[1], a_ref.shape[2]
    m = _stage2_matrix(f2r_ref, f2i_ref, twr_ref, twi_ref)
    a = a_ref[...].reshape(2 * n2, c).astype(BF16)
    x = jnp.dot(m.astype(BF16), a, preferred_element_type=F32)
    xr, xi = x[:n2], x[n2:]
    kr, ki = k_ref[0], k_ref[1]
    y = jnp.concatenate([xr * kr - xi * ki, xr * ki + xi * kr], axis=0).astype(BF16)
    o_ref[...] = jnp.dot(m.T.astype(BF16), y, preferred_element_type=F32).reshape(o_ref.shape)


def spectral_multiply(a, kspec, cst):
    _, N1, N2, C = a.shape
    mat, tw, dat = _spec_blocks(N1, N2, C)
    return pl.pallas_call(
        _spectral_kernel,
        grid=(N1,),
        in_specs=[mat, mat, tw, tw, dat, dat],
        out_specs=dat,
        out_shape=jax.ShapeDtypeStruct(a.shape, F32),
        compiler_params=_params("parallel"),
        name="spectral_multiply",
    )(cst["f2r"], cst["f2i"], cst["twr"], cst["twi"], a, kspec)


def _dft_final_kernel(m_ref, b_ref, u_ref, x0_ref, bias_ref, o_ref):
    rows = m_ref.shape[1]
    c = b_ref.shape[-1]
    m = m_ref[...].astype(BF16)
    b = pltpu.einshape("mnc->nmc", b_ref[...].reshape(rows, SUBLANES, c))
    y = jnp.stack([jnp.dot(m, b[s].astype(BF16), preferred_element_type=F32) for s in range(SUBLANES)], axis=0)
    y = pltpu.einshape("nmc->mnc", y).reshape(o_ref.shape)
    o_ref[...] = (y + u_ref[...] * bias_ref[...]) * x0_ref[...]


def dft_final(b, m3, u4, x04, bias, lo, S, tc=256):
    _, R, N2, C = u4.shape
    assert lo % S == 0
    N1 = b.shape[1]
    tc = _pick(C, tc)
    seq = pl.BlockSpec((S, R, SUBLANES, tc), lambda j, c: (lo // S, 0, j, c))
    return pl.pallas_call(
        _dft_final_kernel,
        grid=(N2 // SUBLANES, C // tc),
        in_specs=[pl.BlockSpec(m3.shape, lambda j, c: (0, 0)),
                  pl.BlockSpec((2, N1, SUBLANES, tc), lambda j, c: (0, 0, j, c)),
                  seq, seq,
                  pl.BlockSpec((1, tc), lambda j, c: (0, c))],
        out_specs=pl.BlockSpec((S, R, SUBLANES, tc), lambda j, c: (0, 0, j, c)),
        out_shape=jax.ShapeDtypeStruct((S, R, N2, C), F32),
        compiler_params=_params("parallel", "parallel"),
        name="dft_final",
    )(m3, b, u4, x04, bias.reshape(1, C))


def long_conv_gate(u, x0, kern, bias, n_pair):
    B, L, C = u.shape
    cst = _dft_consts(L)
    N1, N2, H = cst["N1"], cst["N2"], cst["H"]
    kspec = dft_stage2(dft_stage1(kern.reshape(1, N1, N2, C), cst["m1_full"]), cst)
    u4 = u.reshape(B, H, N2, C)
    x04 = x0.reshape(B, H, N2, C)
    groups = ([(0, 2)] if n_pair == 2 else []) + [(b, 1) for b in range(n_pair, B)]
    outs = []
    for lo, S in groups:
        pair = S == 2
        a = dft_stage1(u4, cst["m1_pair"] if pair else cst["m1_single"], lo, S)
        bsp = spectral_multiply(a, kspec, cst)
        y = dft_final(bsp, cst["m3_pair"] if pair else cst["m3_single"], u4, x04, bias, lo, S)
        outs.append(y.reshape(S * L, C))
    return outs


def _pack_bf16_pairs(h):
    half = h.shape[1] // 2
    bits = lax.bitcast_convert_type(h.astype(BF16).astype(F32), jnp.uint32)
    return bits[:, :half] | (bits[:, half:] >> 16)


def _unpack_bf16_pairs(w):
    hi = lax.bitcast_convert_type(w & jnp.uint32(0xFFFF0000), F32).astype(BF16)
    lo = lax.bitcast_convert_type(w << 16, F32).astype(BF16)
    return hi, lo


def _post_mixer_kernel(xa_ref, xb_ref, a_ref, hya_ref, hyb_ref, ga_ref, gh_ref, wo_ref, gf_ref, wr_ref, br_ref,
                       x1_ref, h_ref, idx_ref, gate_ref, *, n_a):
    from_a = pl.program_id(0) < n_a
    hy = jnp.where(from_a, hya_ref[...], hyb_ref[...])
    merged = jnp.concatenate([_rms(a_ref[...], ga_ref[...]), _rms(hy, gh_ref[...])], axis=-1)
    x = jnp.where(from_a, xa_ref[...], xb_ref[...])
    x1 = x + jnp.dot(merged.astype(BF16), wo_ref[...], preferred_element_type=F32)
    x1_ref[...] = x1
    h = _rms(x1, gf_ref[...])
    h_ref[...] = _pack_bf16_pairs(h)
    logits = lax.dot_general(wr_ref[...], h, (((1,), (1,)), ((), ())),
                             precision=lax.Precision.HIGHEST, preferred_element_type=F32) + br_ref[...]
    e_iota = lax.broadcasted_iota(jnp.int32, logits.shape, 0).astype(F32)
    vals, idxs = [], []
    cur = logits
    for _ in range(TOP_K):
        m = jnp.max(cur, axis=0, keepdims=True)
        sel = jnp.min(jnp.where(cur == m, e_iota, float(N_EXPERTS)), axis=0, keepdims=True)
        vals.append(m)
        idxs.append(sel)
        cur = jnp.where(e_iota == sel, -jnp.inf, cur)
    ex = [jnp.exp(v - vals[0]) for v in vals]
    den = ex[0] + ex[1] + ex[2] + ex[3]
    idx_ref[...] = jnp.concatenate(idxs, axis=0).astype(jnp.int32)
    gate_ref[...] = jnp.concatenate([e / den for e in ex], axis=0)


def post_mixer(xa, xb, attn, hya, hyb, g_attn, g_hy, w_out, g_ffn, w_router_t, b_router, tm=512):
    Ta, D = xa.shape
    T = Ta + xb.shape[0]
    Wa, Wh = attn.shape[1], hya.shape[1]
    assert hya.shape[0] == Ta and hyb.shape[0] == xb.shape[0]
    tm = _pick(math.gcd(Ta, xb.shape[0]), tm)
    n_a = Ta // tm
    spec_a, spec_b = _row_tiles_of_two(tm, D, n_a)
    hy_a, hy_b = _row_tiles_of_two(tm, Wh, n_a)
    const = lambda shape: pl.BlockSpec(shape, lambda i: (0, 0))
    rows = lambda w: pl.BlockSpec((tm, w), lambda i: (i, 0))
    cols = pl.BlockSpec((TOP_K, tm), lambda i: (0, i))
    return pl.pallas_call(
        functools.partial(_post_mixer_kernel, n_a=n_a),
        grid=(T // tm,),
        in_specs=[spec_a, spec_b, rows(Wa), hy_a, hy_b, const((1, Wa)), const((1, Wh)), const((Wa + Wh, D)),
                  const((1, D)), const((N_EXPERTS, D)), const((N_EXPERTS, 1))],
        out_specs=[rows(D), rows(D // 2), cols, cols],
        out_shape=[jax.ShapeDtypeStruct((T, D), F32), jax.ShapeDtypeStruct((T, D // 2), jnp.uint32),
                   jax.ShapeDtypeStruct((TOP_K, T), jnp.int32), jax.ShapeDtypeStruct((TOP_K, T), F32)],
        compiler_params=_params("parallel"),
        name="post_mixer",
    )(xa, xb, attn, hya, hyb, g_attn.reshape(1, Wa), g_hy.reshape(1, Wh), w_out, g_ffn.reshape(1, D),
      w_router_t, b_router.reshape(N_EXPERTS, 1))


def route(top_idx, top_gate, tile):
    K, T = top_idx.shape
    n_assign = K * T
    e_flat = top_idx.reshape(-1)
    pos = jnp.arange(n_assign, dtype=jnp.int32)
    e_sorted, order, gate_sorted = lax.sort((e_flat, pos, top_gate.reshape(-1)), num_keys=1, is_stable=True)
    tok_sorted = order % T
    counts = jnp.sum((e_flat[None, :] == jnp.arange(N_EXPERTS, dtype=jnp.int32)[:, None]).astype(jnp.int32), axis=1)
    start = jnp.cumsum(counts) - counts
    padded = (counts + tile - 1) // tile * tile
    pend = jnp.cumsum(padded)
    pstart = pend - padded
    n_blocks = n_assign // tile + N_EXPERTS
    n_slots = n_blocks * tile
    block_start = jnp.arange(n_blocks, dtype=jnp.int32) * tile
    n_used = (pend[-1] // tile).astype(jnp.int32)
    block_expert = jnp.minimum(jnp.sum(block_start[:, None] >= pend[None, :], axis=1), N_EXPERTS - 1)
    last_expert = block_expert[jnp.maximum(n_used - 1, 0)]
    block_expert = jnp.where(jnp.arange(n_blocks) < n_used, block_expert, last_expert).astype(jnp.int32)
    slot = jnp.arange(n_slots, dtype=jnp.int32)
    slot_e = jnp.repeat(block_expert, tile)
    within = slot - pstart[slot_e]
    valid = jnp.logical_and(within < counts[slot_e], slot < pend[-1])
    src = jnp.clip(start[slot_e] + within, 0, n_assign - 1)
    slot_tok = jnp.where(valid, tok_sorted[src], 0)
    slot_gate = jnp.where(valid, gate_sorted[src], 0.0)
    slot_sorted = pstart[e_sorted] + (pos - start[e_sorted])
    _, slot_of = lax.sort((order, slot_sorted), num_keys=1)
    return slot_tok, slot_gate, slot_of.reshape(K, T), block_expert, n_used.reshape(1)


def _row_copy(src_ref, src_row, dst_ref, dst_row, sem):
    return pltpu.make_async_copy(src_ref.at[pl.ds(src_row, 1), :], dst_ref.at[pl.ds(dst_row, 1), :], sem)


DMA_ISSUE_UNROLL = 8


def _gather_kernel(nrows_ref, idx_ref, src_ref, dst_ref, sem, *, rows):
    live = pl.program_id(0) * rows < nrows_ref[0]

    @pl.when(live)
    def _():
        def issue(r, c):
            _row_copy(src_ref, idx_ref[0, 0, r], dst_ref, r, sem).start()
            return c

        lax.fori_loop(0, rows, issue, 0, unroll=DMA_ISSUE_UNROLL)
        pltpu.make_async_copy(src_ref.at[pl.ds(0, rows), :], dst_ref, sem).wait()

    @pl.when(jnp.logical_not(live))
    def _():
        dst_ref[...] = jnp.zeros(dst_ref.shape, dst_ref.dtype)


def gather_rows(src, idx, n_live_rows, rows=512):
    T, W = src.shape
    n = idx.shape[0]
    rows = _pick(n, rows)
    return pl.pallas_call(
        functools.partial(_gather_kernel, rows=rows),
        grid=(n // rows,),
        in_specs=[pl.BlockSpec(memory_space=pltpu.SMEM),
                  pl.BlockSpec((1, 1, rows), lambda i: (i, 0, 0), memory_space=pltpu.SMEM),
                  pl.BlockSpec(memory_space=pl.ANY)],
        out_specs=pl.BlockSpec((rows, W), lambda i: (i, 0)),
        out_shape=jax.ShapeDtypeStruct((n, W), src.dtype),
        scratch_shapes=[pltpu.SemaphoreType.DMA(())],
        compiler_params=_params("arbitrary"),
        name="gather_rows",
    )(n_live_rows, idx.reshape(n // rows, 1, rows), src)


def _expert_kernel(be_ref, nu_ref, x_ref, wg_ref, wl_ref, bg_ref, bl_ref, wd_ref, bd_ref, g_ref,
                   o_ref, xs_sc, *, n_f):
    b = pl.program_id(0)
    f = pl.program_id(1)
    half = x_ref.shape[1]
    used = b < nu_ref[0]

    @pl.when(jnp.logical_and(used, f == 0))
    def _():
        hi, lo = _unpack_bf16_pairs(x_ref[...])
        xs_sc[:, :half] = hi
        xs_sc[:, half:] = lo

    @pl.when(used)
    def _():
        x = xs_sc[...]
        glu = jnp.dot(x, wg_ref[0], preferred_element_type=F32) + bg_ref[0]
        lin = jnp.dot(x, wl_ref[0], preferred_element_type=F32) + bl_ref[0]
        glu = jnp.minimum(glu, SWIGLU_LIMIT)
        lin = jnp.clip(lin, -SWIGLU_LIMIT, SWIGLU_LIMIT)
        y = glu * jax.nn.sigmoid(SWIGLU_ALPHA * glu) * (lin + 1.0)
        part = jnp.dot(y.astype(BF16), wd_ref[0], preferred_element_type=F32)

        @pl.when(f == 0)
        def _():
            o_ref[...] = part

        @pl.when(f > 0)
        def _():
            o_ref[...] += part

        @pl.when(f == n_f - 1)
        def _():
            g = g_ref[...]
            for c in range(o_ref.shape[1] // LANES):
                sl = slice(c * LANES, (c + 1) * LANES)
                o_ref[:, sl] = (o_ref[:, sl] + bd_ref[0, :, sl]) * g

    @pl.when(jnp.logical_and(jnp.logical_not(used), f == n_f - 1))
    def _():
        o_ref[...] = jnp.zeros(o_ref.shape, o_ref.dtype)


def expert_mlp(x_slots, block_expert, n_used, w_up, b_up, w_down, b_down, slot_gate,
               tile=EXPERT_TILE, tf=FF_TILE):
    n_slots, half = x_slots.shape
    D = 2 * half
    E, _, two_ff = w_up.shape
    d_ff = two_ff // 2
    tf = _pick(d_ff, tf)
    n_f = d_ff // tf
    n_blocks = n_slots // tile
    gate_rep = jnp.broadcast_to(slot_gate[:, None], (n_slots, LANES))

    def blk(b, nu):
        return jnp.minimum(b, jnp.maximum(nu[0] - 1, 0))

    def ff(b, f, nu):
        return jnp.where(b < nu[0], f, n_f - 1)

    grid_spec = pltpu.PrefetchScalarGridSpec(
        num_scalar_prefetch=2,
        grid=(n_blocks, n_f),
        in_specs=[pl.BlockSpec((tile, half), lambda b, f, be, nu: (blk(b, nu), 0)),
                  pl.BlockSpec((1, D, tf), lambda b, f, be, nu: (be[b], 0, ff(b, f, nu))),
                  pl.BlockSpec((1, D, tf), lambda b, f, be, nu: (be[b], 0, n_f + ff(b, f, nu))),
                  pl.BlockSpec((1, 1, tf), lambda b, f, be, nu: (be[b], 0, ff(b, f, nu))),
                  pl.BlockSpec((1, 1, tf), lambda b, f, be, nu: (be[b], 0, n_f + ff(b, f, nu))),
                  pl.BlockSpec((1, tf, D), lambda b, f, be, nu: (be[b], ff(b, f, nu), 0)),
                  pl.BlockSpec((1, 1, D), lambda b, f, be, nu: (be[b], 0, 0)),
                  pl.BlockSpec((tile, LANES), lambda b, f, be, nu: (blk(b, nu), 0))],
        out_specs=pl.BlockSpec((tile, D), lambda b, f, be, nu: (b, 0)),
        scratch_shapes=[pltpu.VMEM((tile, D), BF16)])
    return pl.pallas_call(
        functools.partial(_expert_kernel, n_f=n_f),
        grid_spec=grid_spec,
        out_shape=jax.ShapeDtypeStruct((n_slots, D), F32),
        compiler_params=_params("arbitrary", "arbitrary"),
        name="expert_mlp",
    )(block_expert, n_used, x_slots, w_up, w_up, b_up.reshape(E, 1, two_ff), b_up.reshape(E, 1, two_ff),
      w_down, b_down.reshape(E, 1, D), gate_rep)


def _final_kernel(slot_ref, y_hbm, x1_ref, pa_ref, pb_ref, gp_ref, wg_ref, wp_ref, gf_ref, oa_ref, ob_ref,
                  rows_sc, sem, *, tm, n_a):
    def issue(r, c):
        for k in range(TOP_K):
            _row_copy(y_hbm, slot_ref[0, k, r], rows_sc.at[k], r, sem).start()
        return c

    lax.fori_loop(0, tm, issue, 0, unroll=DMA_ISSUE_UNROLL // TOP_K)
    for k in range(TOP_K):
        pltpu.make_async_copy(y_hbm.at[pl.ds(0, tm), :], rows_sc.at[k], sem).wait()

    x2 = x1_ref[...] + ((rows_sc[0] + rows_sc[1]) + (rows_sc[2] + rows_sc[3]))
    gate = jax.nn.sigmoid(jnp.dot(_rms(x2, gp_ref[...]).astype(BF16), wg_ref[...],
                                  preferred_element_type=F32))
    from_a = pl.program_id(0) < n_a
    p = jnp.where(from_a, pa_ref[...], pb_ref[...])
    x3 = x2 + gate * jnp.dot(p.astype(BF16), wp_ref[...], preferred_element_type=F32)
    y = _rms(x3, gf_ref[...])

    @pl.when(from_a)
    def _():
        oa_ref[...] = y

    @pl.when(jnp.logical_not(from_a))
    def _():
        ob_ref[...] = y


def combine_ple_final(y_slots, slot_of, x1, pa, pb, g_ple, w_gate, w_proj, g_final, tm=256):
    T, D = x1.shape
    Ta, P = pa.shape
    Tb = pb.shape[0]
    tm = _pick(math.gcd(Ta, Tb), tm)
    n_a = Ta // tm
    pa_spec, pb_spec = _row_tiles_of_two(tm, P, n_a)
    oa_spec, ob_spec = _row_tiles_of_two(tm, D, n_a)
    slots = slot_of.reshape(TOP_K, T // tm, tm).transpose(1, 0, 2)
    const = lambda shape: pl.BlockSpec(shape, lambda i: (0, 0))
    return pl.pallas_call(
        functools.partial(_final_kernel, tm=tm, n_a=n_a),
        grid=(T // tm,),
        in_specs=[pl.BlockSpec((1, TOP_K, tm), lambda i: (i, 0, 0), memory_space=pltpu.SMEM),
                  pl.BlockSpec(memory_space=pl.ANY),
                  pl.BlockSpec((tm, D), lambda i: (i, 0)),
                  pa_spec, pb_spec,
                  const((1, D)), const((D, D)), const((P, D)), const((1, D))],
        out_specs=[oa_spec, ob_spec],
        out_shape=[jax.ShapeDtypeStruct((Ta, D), F32), jax.ShapeDtypeStruct((Tb, D), F32)],
        scratch_shapes=[pltpu.VMEM((TOP_K, tm, D), F32),
                        pltpu.SemaphoreType.DMA(())],
        compiler_params=_params("arbitrary"),
        name="combine_ple_final",
    )(slots, y_slots, x1, pa, pb, g_ple.reshape(1, D), w_gate, w_proj, g_final.reshape(1, D))


def kernel(x_prompt, x_sample, p_prompt, p_sample, g_mix, w_in, q_gain, k_gain, conv_w, conv_b, filt_w1, filt_b1, filt_w2, filt_b2, filt_w3, filt_b3, filt_freq, filt_w_out, filt_decay, hyena_bias, g_attn_out, g_hyena_out, w_out, g_ffn, w_router, b_router, w_up, b_up, w_down, b_down, g_ple, w_ple_gate, w_ple_proj, g_final):
    depth = g_mix.shape[0]
    bp, L, D = x_prompt.shape
    bs = x_sample.shape[0]
    B = bp + bs
    T = B * L
    C = hyena_bias.shape[1]
    assert depth == 1
    xa, xb = x_prompt.reshape(bp * L, D), x_sample.reshape(bs * L, D)
    cos_t, sin_t = rope_tables(L)
    n_pair = bp if bp == 2 else 0

    for i in range(depth):
        proj = norm_matmul(xa, xb, g_mix[i], w_in[i].astype(BF16))
        q, k, v = qkv_prep(proj, q_gain[i], k_gain[i], cos_t, sin_t, L)
        attn = flash_attention(q.reshape(B, L, ATTN_WIDTH), k.reshape(B, L, KV_WIDTH),
                               v.reshape(B, L, 2 * KV_WIDTH)).reshape(T, ATTN_WIDTH)
        u, x0 = hyena_pre(proj, conv_w[i], conv_b[i], L)
        kern = long_conv_kernel_rows(L, filt_w1[i], filt_b1[i], filt_w2[i], filt_b2[i], filt_w3[i], filt_b3[i],
                                     filt_freq[i], filt_w_out[i], filt_decay[i])
        assert n_pair == bp and bs == 1
        hya, hyb = long_conv_gate(u.reshape(B, L, C), x0.reshape(B, L, C), kern, hyena_bias[i], n_pair)
        x1, h, top_idx, top_gate = post_mixer(xa, xb, attn, hya, hyb, g_attn_out[i], g_hyena_out[i],
                                              w_out[i].astype(BF16), g_ffn[i], w_router[i].T, b_router[i])
        slot_tok, slot_gate, slot_of, block_expert, n_used = route(top_idx, top_gate, EXPERT_TILE)
        x_slots = gather_rows(h, slot_tok, n_used * EXPERT_TILE)
        y_slots = expert_mlp(x_slots, block_expert, n_used, w_up[i].astype(BF16), b_up[i],
                             w_down[i].astype(BF16), b_down[i], slot_gate)
        pa = p_prompt[i].reshape(bp * L, -1)
        pb = p_sample[i].reshape(bs * L, -1)
        ya, yb = combine_ple_final(y_slots, slot_of, x1, pa, pb, g_ple[i],
                                   w_ple_gate[i].astype(BF16), w_ple_proj[i].astype(BF16), g_final)
    return (ya.reshape(bp, L, D), yb.reshape(bs, L, D))
```

```python
import functools
import math

import numpy as np
import jax
import jax.numpy as jnp
from jax import lax
from jax.experimental import pallas as pl
from jax.experimental.pallas import tpu as pltpu

F32 = jnp.float32
BF16 = jnp.bfloat16

HEAD_DIM = 128
N_Q_HEADS = 8
N_KV_HEADS = 2
Q_PER_KV = N_Q_HEADS // N_KV_HEADS
ATTN_WIDTH = N_Q_HEADS * HEAD_DIM
KV_WIDTH = N_KV_HEADS * HEAD_DIM
GRID_W = 64
ROPE_AXIS_DIM = HEAD_DIM // 2
ROPE_THETA = 10000.0
SHORT_CONV = 3
FILTER_EMB = 33
FILTER_BANDS = (FILTER_EMB - 1) // 2
N_EXPERTS = 32
TOP_K = 4
SWIGLU_ALPHA = 1.702
SWIGLU_LIMIT = 7.0
EPS = 1e-6
LOG2E = 1.4426950408889634

LANES = 128
SUBLANES = 8
VMEM_LIMIT_BYTES = 56 * 1024 * 1024

FFT_N2 = 128
EXPERT_TILE = 512
FF_TILE = 1024


def _params(*sem):
    return pltpu.CompilerParams(dimension_semantics=sem, vmem_limit_bytes=VMEM_LIMIT_BYTES)


def _pick(n, want):
    t = min(n, want)
    while n % t:
        t //= 2
    return t


def _rms(x, g):
    return x * lax.rsqrt(jnp.mean(x * x, axis=-1, keepdims=True) + EPS) * g


def _row_tiles_of_two(tm, width, n_a):
    a = pl.BlockSpec((tm, width), lambda i, *_: (jnp.minimum(i, n_a - 1), 0))
    b = pl.BlockSpec((tm, width), lambda i, *_: (jnp.maximum(i - n_a, 0), 0))
    return a, b


def _norm_matmul_kernel(xa_ref, xb_ref, g_ref, w_ref, o_ref, h_sc, *, n_a):
    @pl.when(pl.program_id(1) == 0)
    def _():
        x = jnp.where(pl.program_id(0) < n_a, xa_ref[...], xb_ref[...])
        h_sc[...] = _rms(x, g_ref[...]).astype(BF16)

    o_ref[...] = jnp.dot(h_sc[...], w_ref[...], preferred_element_type=F32)


def norm_matmul(xa, xb, g, w, tm=1024, tn=512):
    Ta, D = xa.shape
    Tb = xb.shape[0]
    N = w.shape[1]
    tm, tn = _pick(math.gcd(Ta, Tb), tm), _pick(N, tn)
    n_a = Ta // tm
    spec_a, spec_b = _row_tiles_of_two(tm, D, n_a)
    return pl.pallas_call(
        functools.partial(_norm_matmul_kernel, n_a=n_a),
        grid=((Ta + Tb) // tm, N // tn),
        in_specs=[spec_a, spec_b,
                  pl.BlockSpec((1, D), lambda i, j: (0, 0)),
                  pl.BlockSpec((D, tn), lambda i, j: (0, j))],
        out_specs=pl.BlockSpec((tm, tn), lambda i, j: (i, j)),
        out_shape=jax.ShapeDtypeStruct((Ta + Tb, N), F32),
        scratch_shapes=[pltpu.VMEM((tm, D), BF16)],
        compiler_params=_params("parallel", "arbitrary"),
        name="norm_matmul",
    )(xa, xb, g.reshape(1, D), w)


def _qkv_prep_kernel(p_ref, qg_ref, kg_ref, c_ref, s_ref, q_ref, k_ref, v_ref):
    cos = c_ref[...]
    sin = s_ref[...]
    lane = lax.broadcasted_iota(jnp.int32, cos.shape, 1)
    low_half = (lane % ROPE_AXIS_DIM) < (ROPE_AXIS_DIM // 2)

    def norm_rope(x, g):
        y = _rms(x, g)
        fwd = pltpu.roll(y, ROPE_AXIS_DIM // 2, axis=1)
        bwd = pltpu.roll(y, HEAD_DIM - ROPE_AXIS_DIM // 2, axis=1)
        return y * cos + jnp.where(low_half, bwd, fwd) * sin

    q_scale = (HEAD_DIM ** -0.5) * LOG2E
    for h in range(N_Q_HEADS):
        sl = slice(h * HEAD_DIM, (h + 1) * HEAD_DIM)
        q_ref[:, sl] = (norm_rope(p_ref[:, sl], qg_ref[...]) * q_scale).astype(BF16)
    for h in range(N_KV_HEADS):
        src = slice(ATTN_WIDTH + h * HEAD_DIM, ATTN_WIDTH + (h + 1) * HEAD_DIM)
        dst = slice(h * HEAD_DIM, (h + 1) * HEAD_DIM)
        k_ref[:, dst] = norm_rope(p_ref[:, src], kg_ref[...]).astype(BF16)
    ones = jnp.ones((p_ref.shape[0], HEAD_DIM), BF16)
    for h in range(N_KV_HEADS):
        src = slice(ATTN_WIDTH + KV_WIDTH + h * HEAD_DIM, ATTN_WIDTH + KV_WIDTH + (h + 1) * HEAD_DIM)
        v_ref[:, 2 * h * HEAD_DIM:(2 * h + 1) * HEAD_DIM] = p_ref[:, src].astype(BF16)
        v_ref[:, (2 * h + 1) * HEAD_DIM:(2 * h + 2) * HEAD_DIM] = ones


def qkv_prep(proj, q_gain, k_gain, cos_t, sin_t, L, tl=512):
    T = proj.shape[0]
    tl = _pick(L, tl)
    nl = L // tl
    w_qkv = ATTN_WIDTH + 2 * KV_WIDTH
    return pl.pallas_call(
        _qkv_prep_kernel,
        grid=(T // tl,),
        in_specs=[pl.BlockSpec((tl, w_qkv), lambda i: (i, 0)),
                  pl.BlockSpec((1, HEAD_DIM), lambda i: (0, 0)),
                  pl.BlockSpec((1, HEAD_DIM), lambda i: (0, 0)),
                  pl.BlockSpec((tl, HEAD_DIM), lambda i: (i % nl, 0)),
                  pl.BlockSpec((tl, HEAD_DIM), lambda i: (i % nl, 0))],
        out_specs=[pl.BlockSpec((tl, ATTN_WIDTH), lambda i: (i, 0)),
                   pl.BlockSpec((tl, KV_WIDTH), lambda i: (i, 0)),
                   pl.BlockSpec((tl, 2 * KV_WIDTH), lambda i: (i, 0))],
        out_shape=[jax.ShapeDtypeStruct((T, ATTN_WIDTH), BF16),
                   jax.ShapeDtypeStruct((T, KV_WIDTH), BF16),
                   jax.ShapeDtypeStruct((T, 2 * KV_WIDTH), BF16)],
        compiler_params=_params("parallel"),
        name="qkv_prep",
    )(proj, q_gain.reshape(1, HEAD_DIM), k_gain.reshape(1, HEAD_DIM), cos_t, sin_t)


def rope_tables(L):
    rows = L // GRID_W
    row_idx = jnp.repeat(jnp.arange(rows, dtype=F32), GRID_W)
    col_idx = jnp.tile(jnp.arange(GRID_W, dtype=F32), rows)
    inv_freq = ROPE_THETA ** (-jnp.arange(0, ROPE_AXIS_DIM, 2, dtype=F32) / ROPE_AXIS_DIM)
    ar = row_idx[:, None] * inv_freq[None, :]
    ac = col_idx[:, None] * inv_freq[None, :]
    cos_t = jnp.concatenate([jnp.cos(ar), jnp.cos(ar), jnp.cos(ac), jnp.cos(ac)], axis=-1)
    sin_t = jnp.concatenate([-jnp.sin(ar), jnp.sin(ar), -jnp.sin(ac), jnp.sin(ac)], axis=-1)
    return cos_t, sin_t


def _attn_kernel(q_ref, k_ref, v_ref, o_ref, m_sc, acc_sc, *, tk, n_kv):
    m_sc[...] = jnp.full(m_sc.shape, -jnp.inf, F32)
    acc_sc[...] = jnp.zeros(acc_sc.shape, F32)

    def body(j, carry):
        off = pl.multiple_of(j * tk, tk)
        k = k_ref[0, pl.ds(off, tk), :]
        v = v_ref[0, pl.ds(off, tk), :]
        for g in range(Q_PER_KV):
            q = q_ref[0, :, g * HEAD_DIM:(g + 1) * HEAD_DIM]
            s = lax.dot_general(q, k, (((1,), (1,)), ((), ())), preferred_element_type=F32)
            m_prev = m_sc[g]
            m_new = jnp.maximum(m_prev, jnp.max(s, axis=-1, keepdims=True))
            alpha = jnp.exp2(m_prev - m_new)
            p = jnp.exp2(s - jnp.tile(m_new, (1, tk // LANES)))
            pv = jnp.dot(p.astype(BF16), v, preferred_element_type=F32)
            acc_sc[g] = jnp.tile(alpha, (1, 2 * HEAD_DIM // LANES)) * acc_sc[g] + pv
            m_sc[g] = m_new
        return carry

    lax.fori_loop(0, n_kv, body, 0)
    for g in range(Q_PER_KV):
        acc = acc_sc[g]
        o_ref[0, :, g * HEAD_DIM:(g + 1) * HEAD_DIM] = acc[:, :HEAD_DIM] / acc[:, HEAD_DIM:]


def flash_attention(q, k, v1, tq=2048, tk=256):
    B, L, _ = q.shape
    tq, tk = _pick(L, tq), _pick(L, tk)
    gw = Q_PER_KV * HEAD_DIM
    kern = functools.partial(_attn_kernel, tk=tk, n_kv=L // tk)
    return pl.pallas_call(
        kern,
        grid=(B, N_KV_HEADS, L // tq),
        in_specs=[pl.BlockSpec((1, tq, gw), lambda b, h, i: (b, i, h)),
                  pl.BlockSpec((1, L, HEAD_DIM), lambda b, h, i: (b, 0, h)),
                  pl.BlockSpec((1, L, 2 * HEAD_DIM), lambda b, h, i: (b, 0, h))],
        out_specs=pl.BlockSpec((1, tq, gw), lambda b, h, i: (b, i, h)),
        out_shape=jax.ShapeDtypeStruct((B, L, ATTN_WIDTH), F32),
        scratch_shapes=[pltpu.VMEM((Q_PER_KV, tq, LANES), F32),
                        pltpu.VMEM((Q_PER_KV, tq, 2 * HEAD_DIM), F32)],
        compiler_params=_params("parallel", "parallel", "arbitrary"),
        name="flash_attention",
    )(q, k, v1)


def _hyena_pre_kernel(x0_ref, x1_ref, v_ref, x0p_ref, x1p_ref, vp_ref, x0n_ref, x1n_ref, vn_ref,
                      w0_ref, w1_ref, w2_ref, b0_ref, b1_ref, b2_ref, u_ref, x0o_ref, *, n_l):
    i = pl.program_id(0) % n_l
    first = i == 0
    last = i == n_l - 1
    tl = x0_ref.shape[0]
    row = lax.broadcasted_iota(jnp.int32, x0_ref.shape, 0)

    def conv(x_ref, p_ref, n_ref, w_ref, b_ref):
        x = x_ref[...]
        prev_row = jnp.where(first, 0.0, p_ref[SUBLANES - 1:SUBLANES, :])
        next_row = jnp.where(last, 0.0, n_ref[0:1, :])
        xm = jnp.where(row == 0, prev_row, pltpu.roll(x, 1, axis=0))
        xp = jnp.where(row == tl - 1, next_row, pltpu.roll(x, tl - 1, axis=0))
        return xm * w_ref[0:1, :] + x * w_ref[1:2, :] + xp * w_ref[2:3, :] + b_ref[...]

    x0o_ref[...] = conv(x0_ref, x0p_ref, x0n_ref, w0_ref, b0_ref)
    u_ref[...] = conv(v_ref, vp_ref, vn_ref, w2_ref, b2_ref) * conv(x1_ref, x1p_ref, x1n_ref, w1_ref, b1_ref)


def hyena_pre(proj, conv_w, conv_b, L, tl=512, tc=512):
    T = proj.shape[0]
    C = conv_w.shape[1] // 3
    tl, tc = _pick(L, tl), _pick(C, tc)
    n_l = L // tl
    nc = C // tc
    base = (ATTN_WIDTH + 2 * KV_WIDTH) // tc
    r8 = tl // SUBLANES
    n8 = T // SUBLANES

    def main(k):
        return pl.BlockSpec((tl, tc), lambda i, j: (i, base + k * nc + j))

    def prev(k):
        return pl.BlockSpec((SUBLANES, tc), lambda i, j: (jnp.maximum(i * r8 - 1, 0), base + k * nc + j))

    def nxt(k):
        return pl.BlockSpec((SUBLANES, tc), lambda i, j: (jnp.minimum((i + 1) * r8, n8 - 1), base + k * nc + j))

    def wsp(k):
        return pl.BlockSpec((SHORT_CONV, tc), lambda i, j: (0, k * nc + j))

    def bsp(k):
        return pl.BlockSpec((1, tc), lambda i, j: (0, k * nc + j))

    cb = conv_b.reshape(1, 3 * C)
    kern = functools.partial(_hyena_pre_kernel, n_l=n_l)
    return pl.pallas_call(
        kern,
        grid=(T // tl, nc),
        in_specs=[main(0), main(1), main(2), prev(0), prev(1), prev(2), nxt(0), nxt(1), nxt(2),
                  wsp(0), wsp(1), wsp(2), bsp(0), bsp(1), bsp(2)],
        out_specs=[pl.BlockSpec((tl, tc), lambda i, j: (i, j)),
                   pl.BlockSpec((tl, tc), lambda i, j: (i, j))],
        out_shape=[jax.ShapeDtypeStruct((T, C), F32), jax.ShapeDtypeStruct((T, C), F32)],
        compiler_params=_params("parallel", "parallel"),
        name="hyena_pre",
    )(proj, proj, proj, proj, proj, proj, proj, proj, proj,
      conv_w, conv_w, conv_w, cb, cb, cb)


def _filter_kernel(z_ref, t_ref, w1_ref, b1_ref, w2_ref, b2_ref, w3_ref, b3_ref, fr_ref,
                   wo_ref, dec_ref, wob_ref, decb_ref, o_ref, *, n_half):
    i = pl.program_id(0)
    hi = lax.Precision.HIGHEST
    fr = fr_ref[...]
    h = jnp.sin(fr * (jnp.dot(z_ref[...], w1_ref[...], precision=hi, preferred_element_type=F32) + b1_ref[...]))
    h = jnp.sin(fr * (jnp.dot(h, w2_ref[...], precision=hi, preferred_element_type=F32) + b2_ref[...]))
    h = jnp.sin(fr * (jnp.dot(h, w3_ref[...], precision=hi, preferred_element_type=F32) + b3_ref[...]))
    taps = jnp.dot(h, wo_ref[...], precision=hi, preferred_element_type=F32)
    o_ref[...] = taps * jnp.exp(-t_ref[...] * jnp.abs(dec_ref[...]))
    first_row = lax.broadcasted_iota(jnp.int32, (SUBLANES, o_ref.shape[1]), 0) == 0

    @pl.when(i == 0)
    def _():
        hb = jnp.dot(h[:SUBLANES], wob_ref[...], precision=hi, preferred_element_type=F32)
        hb = hb * jnp.exp(-t_ref[0:SUBLANES, :] * jnp.abs(decb_ref[...]))
        o_ref[0:SUBLANES, :] = o_ref[0:SUBLANES, :] + jnp.where(first_row, hb, 0.0)

    @pl.when(i == n_half)
    def _():
        o_ref[0:SUBLANES, :] = jnp.where(first_row, 0.0, o_ref[0:SUBLANES, :])


def filter_features(L):
    t = jnp.linspace(0.0, 1.0, L, dtype=F32)[:, None]
    w = 2.0 * math.pi * jnp.arange(L, dtype=F32) / L
    bands = jnp.linspace(1e-4, FILTER_BANDS - 1, FILTER_BANDS, dtype=F32)
    ang = w[:, None] * bands[None, :]
    z = jnp.concatenate([t, jnp.cos(ang), -jnp.sin(ang)], axis=-1)
    return z, t


def long_conv_kernel_rows(L, w1, b1, w2, b2, w3, b3, freq, w_out, decay, tl=512, tn=1024):
    z, t = filter_features(L)
    mirror = lambda a: jnp.concatenate([a, a[:1], a[:0:-1]], axis=0)
    z, t = mirror(z), mirror(t)
    emb, hid = w1.shape
    emb_pad = -(-emb // LANES) * LANES
    z = jnp.pad(z, ((0, 0), (0, emb_pad - emb)))
    w1 = jnp.pad(w1, ((0, emb_pad - emb), (0, 0)))
    C = w_out.shape[1] // 2
    tl, tn = _pick(L, tl), _pick(C, tn)
    n_half, n_c = L // tl, C // tn
    row = lambda a: a.reshape(1, -1)
    const = lambda shape: pl.BlockSpec(shape, lambda i, j: (0, 0))
    side = lambda i, j: (0, jnp.where(i < n_half, j, n_c + j))
    back = lambda i, j: (0, n_c + j)
    return pl.pallas_call(
        functools.partial(_filter_kernel, n_half=n_half),
        grid=(2 * n_half, n_c),
        in_specs=[pl.BlockSpec((tl, emb_pad), lambda i, j: (i, 0)),
                  pl.BlockSpec((tl, 1), lambda i, j: (i, 0)),
                  const((emb_pad, hid)), const((1, hid)),
                  const((hid, hid)), const((1, hid)),
                  const((hid, hid)), const((1, hid)),
                  const((1, hid)),
                  pl.BlockSpec((hid, tn), side), pl.BlockSpec((1, tn), side),
                  pl.BlockSpec((hid, tn), back), pl.BlockSpec((1, tn), back)],
        out_specs=pl.BlockSpec((tl, tn), lambda i, j: (i, j)),
        out_shape=jax.ShapeDtypeStruct((2 * L, C), F32),
        compiler_params=_params("parallel", "parallel"),
        name="implicit_filter",
    )(z, t, w1, row(b1), w2, row(b2), w3, row(b3), row(freq), w_out, row(decay), w_out, row(decay))


def _dft_consts(L):
    N = 2 * L
    N2 = FFT_N2
    N1 = N // N2
    H = N1 // 2
    k1 = np.arange(N1)
    f1 = np.exp(-2j * np.pi * np.outer(k1, k1) / N1)
    fh = f1[:, :H]
    m1_pair = np.block([[fh.real, -fh.imag], [fh.imag, fh.real]])
    m1_single = np.concatenate([fh.real, fh.imag], axis=0)
    m1_full = np.concatenate([f1.real, f1.imag], axis=0)
    n2 = np.arange(N2)
    f2 = np.exp(-2j * np.pi * np.outer(n2, n2) / N2)
    tw = np.exp(-2j * np.pi * np.outer(k1, n2) / N)
    c = lambda a: jnp.asarray(a, F32)
    return dict(N=N, N1=N1, N2=N2, H=H,
                m1_pair=c(m1_pair), m1_single=c(m1_single), m1_full=c(m1_full),
                m3_pair=c(m1_pair.T / N), m3_single=c(m1_single.T / N),
                f2r=c(f2.real), f2i=c(f2.imag),
                twr=c(tw.real).reshape(N1, 1, N2), twi=c(tw.imag).reshape(N1, 1, N2))


def _dft_stage1_kernel(m_ref, x_ref, o_ref):
    rows = m_ref.shape[1]
    tc = x_ref.shape[-1]
    m = m_ref[...].astype(BF16)
    x = pltpu.einshape("mnc->nmc", x_ref[...].reshape(rows, SUBLANES, tc))
    r = jnp.stack([jnp.dot(m, x[s].astype(BF16), preferred_element_type=F32) for s in range(SUBLANES)], axis=0)
    o_ref[...] = pltpu.einshape("nmc->mnc", r).reshape(o_ref.shape)


def dft_stage1(x4, m1, lo=0, S=None, tc=256):
    B, R, N2, C = x4.shape
    S = B if S is None else S
    assert lo % S == 0 and m1.shape[1] == S * R
    N1 = m1.shape[0] // 2
    tc = _pick(C, tc)
    return pl.pallas_call(
        _dft_stage1_kernel,
        grid=(N2 // SUBLANES, C // tc),
        in_specs=[pl.BlockSpec(m1.shape, lambda j, c: (0, 0)),
                  pl.BlockSpec((S, R, SUBLANES, tc), lambda j, c: (lo // S, 0, j, c))],
        out_specs=pl.BlockSpec((2, N1, SUBLANES, tc), lambda j, c: (0, 0, j, c)),
        out_shape=jax.ShapeDtypeStruct((2, N1, N2, C), F32),
        compiler_params=_params("parallel", "parallel"),
        name="dft_stage1",
    )(m1, x4)


def _stage2_matrix(f2r_ref, f2i_ref, twr_ref, twi_ref):
    f2r, f2i = f2r_ref[...], f2i_ref[...]
    twr, twi = twr_ref[...], twi_ref[...]
    er = f2r * twr - f2i * twi
    ei = f2r * twi + f2i * twr
    return jnp.concatenate([jnp.concatenate([er, -ei], axis=1),
                            jnp.concatenate([ei, er], axis=1)], axis=0)


def _dft_stage2_kernel(f2r_ref, f2i_ref, twr_ref, twi_ref, a_ref, o_ref):
    n2, c = a_ref.shape[1], a_ref.shape[2]
    m = _stage2_matrix(f2r_ref, f2i_ref, twr_ref, twi_ref).astype(BF16)
    a = a_ref[...].reshape(2 * n2, c).astype(BF16)
    o_ref[...] = jnp.dot(m, a, preferred_element_type=F32).reshape(o_ref.shape)


def _spec_blocks(N1, N2, C):
    mat = pl.BlockSpec((N2, N2), lambda i: (0, 0))
    tw = pl.BlockSpec((None, 1, N2), lambda i: (i, 0, 0))
    dat = pl.BlockSpec((2, None, N2, C), lambda i: (0, i, 0, 0))
    return mat, tw, dat


def dft_stage2(a, cst):
    _, N1, N2, C = a.shape
    mat, tw, dat = _spec_blocks(N1, N2, C)
    return pl.pallas_call(
        _dft_stage2_kernel,
        grid=(N1,),
        in_specs=[mat, mat, tw, tw, dat],
        out_specs=dat,
        out_shape=jax.ShapeDtypeStruct(a.shape, F32),
        compiler_params=_params("parallel"),
        name="dft_stage2",
    )(cst["f2r"], cst["f2i"], cst["twr"], cst["twi"], a)


def _spectral_kernel(f2r_ref, f2i_ref, twr_ref, twi_ref, a_ref, k_ref, o_ref):
    n2, c = a_ref.shape[1], a_ref.shape[2]
    m = _stage2_matrix(f2r_ref, f2i_ref, twr_ref, twi_ref)
    a = a_ref[...].reshape(2 * n2, c).astype(BF16)
    x = jnp.dot(m.astype(BF16), a, preferred_element_type=F32)
    xr, xi = x[:n2], x[n2:]
    kr, ki = k_ref[0], k_ref[1]
    y = jnp.concatenate([xr * kr - xi * ki, xr * ki + xi * kr], axis=0).astype(BF16)
    o_ref[...] = jnp.dot(m.T.astype(BF16), y, preferred_element_type=F32).reshape(o_ref.shape)


def spectral_multiply(a, kspec, cst):
    _, N1, N2, C = a.shape
    mat, tw, dat = _spec_blocks(N1, N2, C)
    return pl.pallas_call(
        _spectral_kernel,
        grid=(N1,),
        in_specs=[mat, mat, tw, tw, dat, dat],
        out_specs=dat,
        out_shape=jax.ShapeDtypeStruct(a.shape, F32),
        compiler_params=_params("parallel"),
        name="spectral_multiply",
    )(cst["f2r"], cst["f2i"], cst["twr"], cst["twi"], a, kspec)


def _dft_final_kernel(m_ref, b_ref, u_ref, x0_ref, bias_ref, o_ref):
    rows = m_ref.shape[1]
    c = b_ref.shape[-1]
    m = m_ref[...].astype(BF16)
    b = pltpu.einshape("mnc->nmc", b_ref[...].reshape(rows, SUBLANES, c))
    y = jnp.stack([jnp.dot(m, b[s].astype(BF16), preferred_element_type=F32) for s in range(SUBLANES)], axis=0)
    y = pltpu.einshape("nmc->mnc", y).reshape(o_ref.shape)
    o_ref[...] = (y + u_ref[...] * bias_ref[...]) * x0_ref[...]


def dft_final(b, m3, u4, x04, bias, lo, S, tc=256):
    _, R, N2, C = u4.shape
    assert lo % S == 0
    N1 = b.shape[1]
    tc = _pick(C, tc)
    seq = pl.BlockSpec((S, R, SUBLANES, tc), lambda j, c: (lo // S, 0, j, c))
    return pl.pallas_call(
        _dft_final_kernel,
        grid=(N2 // SUBLANES, C // tc),
        in_specs=[pl.BlockSpec(m3.shape, lambda j, c: (0, 0)),
                  pl.BlockSpec((2, N1, SUBLANES, tc), lambda j, c: (0, 0, j, c)),
                  seq, seq,
                  pl.BlockSpec((1, tc), lambda j, c: (0, c))],
        out_specs=pl.BlockSpec((S, R, SUBLANES, tc), lambda j, c: (0, 0, j, c)),
        out_shape=jax.ShapeDtypeStruct((S, R, N2, C), F32),
        compiler_params=_params("parallel", "parallel"),
        name="dft_final",
    )(m3, b, u4, x04, bias.reshape(1, C))


def long_conv_gate(u, x0, kern, bias, n_pair):
    B, L, C = u.shape
    cst = _dft_consts(L)
    N1, N2, H = cst["N1"], cst["N2"], cst["H"]
    kspec = dft_stage2(dft_stage1(kern.reshape(1, N1, N2, C), cst["m1_full"]), cst)
    u4 = u.reshape(B, H, N2, C)
    x04 = x0.reshape(B, H, N2, C)
    groups = ([(0, 2)] if n_pair == 2 else []) + [(b, 1) for b in range(n_pair, B)]
    outs = []
    for lo, S in groups:
        pair = S == 2
        a = dft_stage1(u4, cst["m1_pair"] if pair else cst["m1_single"], lo, S)
        bsp = spectral_multiply(a, kspec, cst)
        y = dft_final(bsp, cst["m3_pair"] if pair else cst["m3_single"], u4, x04, bias, lo, S)
        outs.append(y.reshape(S * L, C))
    return outs


def _pack_bf16_pairs(h):
    half = h.shape[1] // 2
    bits = lax.bitcast_convert_type(h.astype(BF16).astype(F32), jnp.uint32)
    return bits[:, :half] | (bits[:, half:] >> 16)


def _unpack_bf16_pairs(w):
    hi = lax.bitcast_convert_type(w & jnp.uint32(0xFFFF0000), F32).astype(BF16)
    lo = lax.bitcast_convert_type(w << 16, F32).astype(BF16)
    return hi, lo


def _post_mixer_kernel(xa_ref, xb_ref, a_ref, hya_ref, hyb_ref, ga_ref, gh_ref, wo_ref, gf_ref, wr_ref, br_ref,
                       x1_ref, h_ref, idx_ref, gate_ref, *, n_a):
    from_a = pl.program_id(0) < n_a
    hy = jnp.where(from_a, hya_ref[...], hyb_ref[...])
    merged = jnp.concatenate([_rms(a_ref[...], ga_ref[...]), _rms(hy, gh_ref[...])], axis=-1)
    x = jnp.where(from_a, xa_ref[...], xb_ref[...])
    x1 = x + jnp.dot(merged.astype(BF16), wo_ref[...], preferred_element_type=F32)
    x1_ref[...] = x1
    h = _rms(x1, gf_ref[...])
    h_ref[...] = _pack_bf16_pairs(h)
    logits = lax.dot_general(wr_ref[...], h, (((1,), (1,)), ((), ())),
                             precision=lax.Precision.HIGHEST, preferred_element_type=F32) + br_ref[...]
    e_iota = lax.broadcasted_iota(jnp.int32, logits.shape, 0).astype(F32)
    vals, idxs = [], []
    cur = logits
    for _ in range(TOP_K):
        m = jnp.max(cur, axis=0, keepdims=True)
        sel = jnp.min(jnp.where(cur == m, e_iota, float(N_EXPERTS)), axis=0, keepdims=True)
        vals.append(m)
        idxs.append(sel)
        cur = jnp.where(e_iota == sel, -jnp.inf, cur)
    ex = [jnp.exp(v - vals[0]) for v in vals]
    den = ex[0] + ex[1] + ex[2] + ex[3]
    idx_ref[...] = jnp.concatenate(idxs, axis=0).astype(jnp.int32)
    gate_ref[...] = jnp.concatenate([e / den for e in ex], axis=0)


def post_mixer(xa, xb, attn, hya, hyb, g_attn, g_hy, w_out, g_ffn, w_router_t, b_router, tm=512):
    Ta, D = xa.shape
    T = Ta + xb.shape[0]
    Wa, Wh = attn.shape[1], hya.shape[1]
    assert hya.shape[0] == Ta and hyb.shape[0] == xb.shape[0]
    tm = _pick(math.gcd(Ta, xb.shape[0]), tm)
    n_a = Ta // tm
    spec_a, spec_b = _row_tiles_of_two(tm, D, n_a)
    hy_a, hy_b = _row_tiles_of_two(tm, Wh, n_a)
    const = lambda shape: pl.BlockSpec(shape, lambda i: (0, 0))
    rows = lambda w: pl.BlockSpec((tm, w), lambda i: (i, 0))
    cols = pl.BlockSpec((TOP_K, tm), lambda i: (0, i))
    return pl.pallas_call(
        functools.partial(_post_mixer_kernel, n_a=n_a),
        grid=(T // tm,),
        in_specs=[spec_a, spec_b, rows(Wa), hy_a, hy_b, const((1, Wa)), const((1, Wh)), const((Wa + Wh, D)),
                  const((1, D)), const((N_EXPERTS, D)), const((N_EXPERTS, 1))],
        out_specs=[rows(D), rows(D // 2), cols, cols],
        out_shape=[jax.ShapeDtypeStruct((T, D), F32), jax.ShapeDtypeStruct((T, D // 2), jnp.uint32),
                   jax.ShapeDtypeStruct((TOP_K, T), jnp.int32), jax.ShapeDtypeStruct((TOP_K, T), F32)],
        compiler_params=_params("parallel"),
        name="post_mixer",
    )(xa, xb, attn, hya, hyb, g_attn.reshape(1, Wa), g_hy.reshape(1, Wh), w_out, g_ffn.reshape(1, D),
      w_router_t, b_router.reshape(N_EXPERTS, 1))


def route(top_idx, top_gate, tile):
    K, T = top_idx.shape
    n_assign = K * T
    e_flat = top_idx.reshape(-1)
    pos = jnp.arange(n_assign, dtype=jnp.int32)
    e_sorted, order, gate_sorted = lax.sort((e_flat, pos, top_gate.reshape(-1)), num_keys=1, is_stable=True)
    tok_sorted = order % T
    counts = jnp.sum((e_flat[None, :] == jnp.arange(N_EXPERTS, dtype=jnp.int32)[:, None]).astype(jnp.int32), axis=1)
    start = jnp.cumsum(counts) - counts
    padded = (counts + tile - 1) // tile * tile
    pend = jnp.cumsum(padded)
    pstart = pend - padded
    n_blocks = n_assign // tile + N_EXPERTS
    n_slots = n_blocks * tile
    block_start = jnp.arange(n_blocks, dtype=jnp.int32) * tile
    n_used = (pend[-1] // tile).astype(jnp.int32)
    block_expert = jnp.minimum(jnp.sum(block_start[:, None] >= pend[None, :], axis=1), N_EXPERTS - 1)
    last_expert = block_expert[jnp.maximum(n_used - 1, 0)]
    block_expert = jnp.where(jnp.arange(n_blocks) < n_used, block_expert, last_expert).astype(jnp.int32)
    slot = jnp.arange(n_slots, dtype=jnp.int32)
    slot_e = jnp.repeat(block_expert, tile)
    within = slot - pstart[slot_e]
    valid = jnp.logical_and(within < counts[slot_e], slot < pend[-1])
    src = jnp.clip(start[slot_e] + within, 0, n_assign - 1)
    slot_tok = jnp.where(valid, tok_sorted[src], 0)
    slot_gate = jnp.where(valid, gate_sorted[src], 0.0)
    slot_sorted = pstart[e_sorted] + (pos - start[e_sorted])
    _, slot_of = lax.sort((order, slot_sorted), num_keys=1)
    return slot_tok, slot_gate, slot_of.reshape(K, T), block_expert, n_used.reshape(1)


def _row_copy(src_ref, src_row, dst_ref, dst_row, sem):
    return pltpu.make_async_copy(src_ref.at[pl.ds(src_row, 1), :], dst_ref.at[pl.ds(dst_row, 1), :], sem)


DMA_ISSUE_UNROLL = 8


def _gather_kernel(nrows_ref, idx_ref, src_ref, dst_ref, sem, *, rows):
    live = pl.program_id(0) * rows < nrows_ref[0]

    @pl.when(live)
    def _():
        def issue(r2, c):
            for par in range(2):
                r = 2 * r2 + par
                _row_copy(src_ref, idx_ref[0, 0, r], dst_ref, r, sem).start(priority=par)
            return c

        lax.fori_loop(0, rows // 2, issue, 0, unroll=DMA_ISSUE_UNROLL // 2)
        pltpu.make_async_copy(src_ref.at[pl.ds(0, rows), :], dst_ref, sem).wait()

    @pl.when(jnp.logical_not(live))
    def _():
        dst_ref[...] = jnp.zeros(dst_ref.shape, dst_ref.dtype)


def gather_rows(src, idx, n_live_rows, rows=512):
    T, W = src.shape
    n = idx.shape[0]
    rows = _pick(n, rows)
    return pl.pallas_call(
        functools.partial(_gather_kernel, rows=rows),
        grid=(n // rows,),
        in_specs=[pl.BlockSpec(memory_space=pltpu.SMEM),
                  pl.BlockSpec((1, 1, rows), lambda i: (i, 0, 0), memory_space=pltpu.SMEM),
                  pl.BlockSpec(memory_space=pl.ANY)],
        out_specs=pl.BlockSpec((rows, W), lambda i: (i, 0)),
        out_shape=jax.ShapeDtypeStruct((n, W), src.dtype),
        scratch_shapes=[pltpu.SemaphoreType.DMA(())],
        compiler_params=_params("arbitrary"),
        name="gather_rows",
    )(n_live_rows, idx.reshape(n // rows, 1, rows), src)


def _expert_kernel(be_ref, nu_ref, x_ref, wg_ref, wl_ref, bg_ref, bl_ref, wd_ref, bd_ref, g_ref,
                   o_ref, xs_sc, *, n_f):
    b = pl.program_id(0)
    f = pl.program_id(1)
    half = x_ref.shape[1]
    used = b < nu_ref[0]

    @pl.when(jnp.logical_and(used, f == 0))
    def _():
        hi, lo = _unpack_bf16_pairs(x_ref[...])
        xs_sc[:, :half] = hi
        xs_sc[:, half:] = lo

    @pl.when(used)
    def _():
        x = xs_sc[...]
        glu = jnp.dot(x, wg_ref[0], preferred_element_type=F32) + bg_ref[0]
        lin = jnp.dot(x, wl_ref[0], preferred_element_type=F32) + bl_ref[0]
        glu = jnp.minimum(glu, SWIGLU_LIMIT)
        lin = jnp.clip(lin, -SWIGLU_LIMIT, SWIGLU_LIMIT)
        y = glu * jax.nn.sigmoid(SWIGLU_ALPHA * glu) * (lin + 1.0)
        part = jnp.dot(y.astype(BF16), wd_ref[0], preferred_element_type=F32)

        @pl.when(f == 0)
        def _():
            o_ref[...] = part

        @pl.when(f > 0)
        def _():
            o_ref[...] += part

        @pl.when(f == n_f - 1)
        def _():
            g = g_ref[...]
            for c in range(o_ref.shape[1] // LANES):
                sl = slice(c * LANES, (c + 1) * LANES)
                o_ref[:, sl] = (o_ref[:, sl] + bd_ref[0, :, sl]) * g

    @pl.when(jnp.logical_and(jnp.logical_not(used), f == n_f - 1))
    def _():
        o_ref[...] = jnp.zeros(o_ref.shape, o_ref.dtype)


def expert_mlp(x_slots, block_expert, n_used, w_up, b_up, w_down, b_down, slot_gate,
               tile=EXPERT_TILE, tf=FF_TILE):
    n_slots, half = x_slots.shape
    D = 2 * half
    E, _, two_ff = w_up.shape
    d_ff = two_ff // 2
    tf = _pick(d_ff, tf)
    n_f = d_ff // tf
    n_blocks = n_slots // tile
    gate_rep = jnp.broadcast_to(slot_gate[:, None], (n_slots, LANES))

    def blk(b, nu):
        return jnp.minimum(b, jnp.maximum(nu[0] - 1, 0))

    def ff(b, f, nu):
        return jnp.where(b < nu[0], f, n_f - 1)

    grid_spec = pltpu.PrefetchScalarGridSpec(
        num_scalar_prefetch=2,
        grid=(n_blocks, n_f),
        in_specs=[pl.BlockSpec((tile, half), lambda b, f, be, nu: (blk(b, nu), 0)),
                  pl.BlockSpec((1, D, tf), lambda b, f, be, nu: (be[b], 0, ff(b, f, nu))),
                  pl.BlockSpec((1, D, tf), lambda b, f, be, nu: (be[b], 0, n_f + ff(b, f, nu))),
                  pl.BlockSpec((1, 1, tf), lambda b, f, be, nu: (be[b], 0, ff(b, f, nu))),
                  pl.BlockSpec((1, 1, tf), lambda b, f, be, nu: (be[b], 0, n_f + ff(b, f, nu))),
                  pl.BlockSpec((1, tf, D), lambda b, f, be, nu: (be[b], ff(b, f, nu), 0)),
                  pl.BlockSpec((1, 1, D), lambda b, f, be, nu: (be[b], 0, 0)),
                  pl.BlockSpec((tile, LANES), lambda b, f, be, nu: (blk(b, nu), 0))],
        out_specs=pl.BlockSpec((tile, D), lambda b, f, be, nu: (b, 0)),
        scratch_shapes=[pltpu.VMEM((tile, D), BF16)])
    return pl.pallas_call(
        functools.partial(_expert_kernel, n_f=n_f),
        grid_spec=grid_spec,
        out_shape=jax.ShapeDtypeStruct((n_slots, D), F32),
        compiler_params=_params("arbitrary", "arbitrary"),
        name="expert_mlp",
    )(block_expert, n_used, x_slots, w_up, w_up, b_up.reshape(E, 1, two_ff), b_up.reshape(E, 1, two_ff),
      w_down, b_down.reshape(E, 1, D), gate_rep)


def _final_kernel(slot_ref, y_hbm, x1_ref, pa_ref, pb_ref, gp_ref, wg_ref, wp_ref, gf_ref, oa_ref, ob_ref,
                  rows_sc, sem, *, tm, n_a):
    def issue(r, c):
        for k in range(TOP_K):
            _row_copy(y_hbm, slot_ref[0, k, r], rows_sc.at[k], r, sem).start(priority=k % 2)
        return c

    lax.fori_loop(0, tm, issue, 0, unroll=DMA_ISSUE_UNROLL // TOP_K)
    for k in range(TOP_K):
        pltpu.make_async_copy(y_hbm.at[pl.ds(0, tm), :], rows_sc.at[k], sem).wait()

    x2 = x1_ref[...] + ((rows_sc[0] + rows_sc[1]) + (rows_sc[2] + rows_sc[3]))
    gate = jax.nn.sigmoid(jnp.dot(_rms(x2, gp_ref[...]).astype(BF16), wg_ref[...],
                                  preferred_element_type=F32))
    from_a = pl.program_id(0) < n_a
    p = jnp.where(from_a, pa_ref[...], pb_ref[...])
    x3 = x2 + gate * jnp.dot(p.astype(BF16), wp_ref[...], preferred_element_type=F32)
    y = _rms(x3, gf_ref[...])

    @pl.when(from_a)
    def _():
        oa_ref[...] = y

    @pl.when(jnp.logical_not(from_a))
    def _():
        ob_ref[...] = y


def combine_ple_final(y_slots, slot_of, x1, pa, pb, g_ple, w_gate, w_proj, g_final, tm=256):
    T, D = x1.shape
    Ta, P = pa.shape
    Tb = pb.shape[0]
    tm = _pick(math.gcd(Ta, Tb), tm)
    n_a = Ta // tm
    pa_spec, pb_spec = _row_tiles_of_two(tm, P, n_a)
    oa_spec, ob_spec = _row_tiles_of_two(tm, D, n_a)
    slots = slot_of.reshape(TOP_K, T // tm, tm).transpose(1, 0, 2)
    const = lambda shape: pl.BlockSpec(shape, lambda i: (0, 0))
    return pl.pallas_call(
        functools.partial(_final_kernel, tm=tm, n_a=n_a),
        grid=(T // tm,),
        in_specs=[pl.BlockSpec((1, TOP_K, tm), lambda i: (i, 0, 0), memory_space=pltpu.SMEM),
                  pl.BlockSpec(memory_space=pl.ANY),
                  pl.BlockSpec((tm, D), lambda i: (i, 0)),
                  pa_spec, pb_spec,
                  const((1, D)), const((D, D)), const((P, D)), const((1, D))],
        out_specs=[oa_spec, ob_spec],
        out_shape=[jax.ShapeDtypeStruct((Ta, D), F32), jax.ShapeDtypeStruct((Tb, D), F32)],
        scratch_shapes=[pltpu.VMEM((TOP_K, tm, D), F32),
                        pltpu.SemaphoreType.DMA(())],
        compiler_params=_params("arbitrary"),
        name="combine_ple_final",
    )(slots, y_slots, x1, pa, pb, g_ple.reshape(1, D), w_gate, w_proj, g_final.reshape(1, D))


def kernel(x_prompt, x_sample, p_prompt, p_sample, g_mix, w_in, q_gain, k_gain, conv_w, conv_b, filt_w1, filt_b1, filt_w2, filt_b2, filt_w3, filt_b3, filt_freq, filt_w_out, filt_decay, hyena_bias, g_attn_out, g_hyena_out, w_out, g_ffn, w_router, b_router, w_up, b_up, w_down, b_down, g_ple, w_ple_gate, w_ple_proj, g_final):
    depth = g_mix.shape[0]
    bp, L, D = x_prompt.shape
    bs = x_sample.shape[0]
    B = bp + bs
    T = B * L
    C = hyena_bias.shape[1]
    assert depth == 1
    xa, xb = x_prompt.reshape(bp * L, D), x_sample.reshape(bs * L, D)
    cos_t, sin_t = rope_tables(L)
    n_pair = bp if bp == 2 else 0

    for i in range(depth):
        proj = norm_matmul(xa, xb, g_mix[i], w_in[i].astype(BF16))
        q, k, v = qkv_prep(proj, q_gain[i], k_gain[i], cos_t, sin_t, L)
        attn = flash_attention(q.reshape(B, L, ATTN_WIDTH), k.reshape(B, L, KV_WIDTH),
                               v.reshape(B, L, 2 * KV_WIDTH)).reshape(T, ATTN_WIDTH)
        u, x0 = hyena_pre(proj, conv_w[i], conv_b[i], L)
        kern = long_conv_kernel_rows(L, filt_w1[i], filt_b1[i], filt_w2[i], filt_b2[i], filt_w3[i], filt_b3[i],
                                     filt_freq[i], filt_w_out[i], filt_decay[i])
        assert n_pair == bp and bs == 1
        hya, hyb = long_conv_gate(u.reshape(B, L, C), x0.reshape(B, L, C), kern, hyena_bias[i], n_pair)
        x1, h, top_idx, top_gate = post_mixer(xa, xb, attn, hya, hyb, g_attn_out[i], g_hyena_out[i],
                                              w_out[i].astype(BF16), g_ffn[i], w_router[i].T, b_router[i])
        slot_tok, slot_gate, slot_of, block_expert, n_used = route(top_idx, top_gate, EXPERT_TILE)
        x_slots = gather_rows(h, slot_tok, n_used * EXPERT_TILE)
        y_slots = expert_mlp(x_slots, block_expert, n_used, w_up[i].astype(BF16), b_up[i],
                             w_down[i].astype(BF16), b_down[i], slot_gate)
        pa = p_prompt[i].reshape(bp * L, -1)
        pb = p_sample[i].reshape(bs * L, -1)
        ya, yb = combine_ple_final(y_slots, slot_of, x1, pa, pb, g_ple[i],
                                   w_ple_gate[i].astype(BF16), w_ple_proj[i].astype(BF16), g_final)
    return (ya.reshape(bp, L, D), yb.reshape(bs, L, D))
```
